```python
import jax, jax.numpy as jnp
from jax import lax
import numpy as np

D_MODEL = 4096
BATCH = 16
SEQ = 256
DEPTH = 2
DEC_BATCH = 4
DEC_SEQ = 1024
PAST_LEN = 512

GRID_W = 64
HEAD_DIM = 128
ATT_WIDTH = D_MODEL // 2
ATT_HEADS = ATT_WIDTH // HEAD_DIM
ATT_KV_HEADS = ATT_HEADS // 4
KV_WIDTH = ATT_KV_HEADS * HEAD_DIM
Q_BLOCK = 128
ROPE_THETA = 10000.0
ROPE_FREQS = HEAD_DIM // 4
RWKV_WIDTH = D_MODEL // 4
RWKV_HEAD = 64
RWKV_HEADS = RWKV_WIDTH // RWKV_HEAD
DECAY_LORA = 64
ICLR_LORA = 64
GATE_LORA = 160
RWKV_IN = 3 * RWKV_WIDTH + 2 * DECAY_LORA + 2 * ICLR_LORA + GATE_LORA
SGU_WIDTH = D_MODEL // 4
SGU_GROUP_DIM = 128
SGU_GROUPS = SGU_WIDTH // SGU_GROUP_DIM
CHUNK = 128
MIX_WIDTH = ATT_WIDTH + RWKV_WIDTH + SGU_WIDTH
IN_WIDTH = ATT_WIDTH + 2 * KV_WIDTH + RWKV_IN + 2 * SGU_WIDTH
N_EXPERTS = 32
TOP_K = 4
EXPERT_FF = D_MODEL // 2
SWIGLU_LIMIT = 7.0
SWIGLU_ALPHA = 1.702
NORM_EPS = 1e-6
LNX_EPS = 64e-5
L2_EPS = 1e-12

kernel_name = 'hybrid_dit_rwkv7_gqa_sgu_moe_step'


def _split(x, widths):
    return jnp.split(x, [int(i) for i in np.cumsum(widths)[:-1]], axis=-1)


def _rmsnorm(x, g):
    xf = x.astype(jnp.float32)
    y = xf * lax.rsqrt(jnp.mean(xf * xf, axis=-1, keepdims=True) + NORM_EPS)
    return (y * g.astype(jnp.float32)).astype(x.dtype)


def _centred_shift(p):
    prev = jnp.pad(p[:, :-1], ((0, 0), (1, 0), (0, 0)))
    nxt = jnp.pad(p[:, 1:], ((0, 0), (0, 1), (0, 0)))
    return 0.5 * (prev + nxt)


def _axial_rope(rows):
    row = jnp.repeat(jnp.arange(rows), GRID_W).astype(jnp.float32)
    col = jnp.tile(jnp.arange(GRID_W), rows).astype(jnp.float32)
    inv = ROPE_THETA ** (-jnp.arange(ROPE_FREQS, dtype=jnp.float32) / ROPE_FREQS)
    ang = jnp.stack([row[:, None] * inv, col[:, None] * inv], axis=1)
    return jnp.cos(ang), jnp.sin(ang)


def _apply_rope(x, cos, sin):
    B, T, H, D = x.shape
    xs = x.astype(jnp.float32).reshape(B, T, H, 2, 2, ROPE_FREQS)
    x1, x2 = xs[..., 0, :], xs[..., 1, :]
    c, s = cos[None, :, None], sin[None, :, None]
    out = jnp.stack([x1 * c - x2 * s, x2 * c + x1 * s], axis=-2)
    return out.reshape(B, T, H, D).astype(x.dtype)


def _attend_blocks(q, k, v):
    B, T, H, D = q.shape
    G = H // ATT_KV_HEADS
    qb = jnp.moveaxis(q.reshape(B, T // Q_BLOCK, Q_BLOCK, ATT_KV_HEADS, G, D), 1, 0)

    def block(qblk):
        s = jnp.einsum('bqkgd,blkd->bkgql', qblk, k, preferred_element_type=jnp.float32) * (D ** -0.5)
        p = jax.nn.softmax(s, axis=-1).astype(v.dtype)
        return jnp.einsum('bkgql,blkd->bqkgd', p, v)

    out = lax.map(block, qb)
    return jnp.moveaxis(out, 0, 1).reshape(B, T, H * D)


def _wkv_scan(s0, r, w, k, v, a, b, reverse):
    def step(S, inp):
        r_t, w_t, k_t, v_t, a_t, b_t = inp
        sa = jnp.einsum('bhij,bhj->bhi', S, a_t)
        S = S * w_t[:, :, None, :] + sa[..., None] * b_t[:, :, None, :] + v_t[..., None] * k_t[:, :, None, :]
        return S, jnp.einsum('bhij,bhj->bhi', S, r_t)

    xs = tuple(jnp.moveaxis(t, 1, 0) for t in (r, w, k, v, a, b))
    s_fin, ys = lax.scan(step, s0, xs, reverse=reverse)
    return s_fin, jnp.moveaxis(ys, 0, 1)


def _rwkv(prw, lp, s_f0, s_b0):
    B, T, _ = prw.shape
    p = (prw + (_centred_shift(prw) - prw) * lp['rwkv_mu']).astype(jnp.float32)
    r, k, v, zw_f, zw_b, za_f, za_b, zg = _split(
        p, [RWKV_WIDTH] * 3 + [DECAY_LORA] * 2 + [ICLR_LORA] * 2 + [GATE_LORA])
    heads = lambda t: t.reshape(B, T, RWKV_HEADS, RWKV_HEAD)
    kk = heads(k * lp['k_k'])
    kk = kk / jnp.maximum(jnp.sqrt(jnp.sum(kk * kk, axis=-1, keepdims=True)), L2_EPS)
    rh, vh = heads(r), heads(v)
    y = jnp.zeros_like(rh)
    bonus = jnp.zeros_like(rh)
    finals = []
    for d, (zw, za, s0) in enumerate(((zw_f, za_f, s_f0), (zw_b, za_b, s_b0))):
        w_log = -jax.nn.softplus(-(lp['w0'][d] + jnp.tanh(zw) @ lp['w_up'][d])) - 0.5
        decay = jnp.exp(-jnp.exp(w_log))
        a = jax.nn.sigmoid(lp['a0'][d] + za @ lp['a_up'][d])
        kd = heads(k * (1.0 + (a - 1.0) * lp['k_a']))
        s_fin, y_d = _wkv_scan(s0.astype(jnp.float32), rh, heads(decay), kd, vh, -kk, kk * heads(a), d == 1)
        y = y + y_d
        bonus = bonus + jnp.sum(rh * kd * lp['r_k'], axis=-1, keepdims=True) * vh
        finals.append(s_fin)
    mean = jnp.mean(y, axis=-1, keepdims=True)
    var = jnp.mean(jnp.square(y - mean), axis=-1, keepdims=True)
    yn = ((y - mean) * lax.rsqrt(var + LNX_EPS)).reshape(B, T, RWKV_WIDTH) * lp['lnx_w'] + lp['lnx_b']
    g = jax.nn.sigmoid(zg) @ lp['g_up']
    out = (yn + bonus.reshape(B, T, RWKV_WIDTH)) * g
    return out.astype(prw.dtype), finals[0], finals[1]


def _sgu(u, sv, lp):
    B, T, _ = u.shape
    nc = T // CHUNK
    shape = (B, nc, CHUNK, SGU_GROUPS, SGU_GROUP_DIM)
    svn = _rmsnorm(sv.reshape(shape), lp['sgu_norm'].reshape(SGU_GROUPS, SGU_GROUP_DIM))
    z = jnp.einsum('gpq,bcqgd->bcpgd', lp['sgu_w'], svn) + lp['sgu_b'].T[:, :, None]
    return (u.reshape(shape) * z).reshape(B, T, SGU_WIDTH)


def _moe(h, lp, experts, layer):
    w_exp1, b_exp1, w_exp2, b_exp2 = experts
    B, T, D = h.shape
    hf = h.reshape(B * T, D)
    logits = jnp.dot(hf, lp['w_router'], preferred_element_type=jnp.float32) + lp['b_router']
    top_v, top_i = lax.top_k(logits, TOP_K)
    probs = jax.nn.softmax(top_v, axis=-1)
    gates = jnp.einsum('nk,nke->ne', probs, jax.nn.one_hot(top_i, N_EXPERTS, dtype=jnp.float32))
    y = jnp.zeros((B * T, D), jnp.float32)
    for e in range(N_EXPERTS):
        z = (hf @ w_exp1[layer, e] + b_exp1[layer, e]).astype(jnp.float32)
        z_glu, z_lin = jnp.split(z, 2, axis=-1)
        z_glu = jnp.minimum(z_glu, SWIGLU_LIMIT)
        z_lin = jnp.clip(z_lin, -SWIGLU_LIMIT, SWIGLU_LIMIT)
        act = (z_glu * jax.nn.sigmoid(SWIGLU_ALPHA * z_glu) * (z_lin + 1.0)).astype(h.dtype)
        y = y + gates[:, e:e + 1] * (act @ w_exp2[layer, e] + b_exp2[layer, e])
    return y.astype(h.dtype).reshape(B, T, D)


def _layer(x, cond, lp, experts, layer, ctx, rope):
    B, T, _ = x.shape
    mod = (jax.nn.silu(cond) @ lp['w_mod'] + lp['b_mod'])[:, None, :]
    sh1, sc1, g1, sh2, sc2, g2 = jnp.split(mod, 6, axis=-1)
    h = _rmsnorm(x, lp['norm_mix']) * (1.0 + sc1) + sh1
    pq, pk, pv, prw, pu, psv = _split(
        h @ lp['w_in'], [ATT_WIDTH, KV_WIDTH, KV_WIDTH, RWKV_IN, SGU_WIDTH, SGU_WIDTH])
    q = _rmsnorm(pq.reshape(B, T, ATT_HEADS, HEAD_DIM), lp['q_norm'])
    k = _rmsnorm(pk.reshape(B, T, ATT_KV_HEADS, HEAD_DIM), lp['k_norm'])
    v = pv.reshape(B, T, ATT_KV_HEADS, HEAD_DIM)
    if ctx is None:
        k_all, v_all = k, v
        s_f0 = jnp.zeros((B, RWKV_HEADS, RWKV_HEAD, RWKV_HEAD), jnp.float32)
        s_b0 = s_f0
    else:
        k_ctx, v_ctx, s_f0, s_b0 = ctx
        q = _apply_rope(q, rope[0], rope[1])
        k = _apply_rope(k, rope[0], rope[1])
        k_all = jnp.concatenate([k_ctx.astype(k.dtype), k], axis=1)
        v_all = jnp.concatenate([v_ctx.astype(v.dtype), v], axis=1)
    att = _attend_blocks(q, k_all, v_all)
    rw, s_f, s_b = _rwkv(prw, lp, s_f0, s_b0)
    sg = _sgu(pu, psv, lp)
    mix = jnp.concatenate([att, rw, sg], axis=-1) @ lp['w_out']
    x = x + g1 * mix
    h2 = _rmsnorm(x, lp['norm_ffn']) * (1.0 + sc2) + sh2
    x = x + g2 * _moe(h2, lp, experts, layer)
    return x, k, v, s_f, s_b


def setup_inputs(seed: int = 0) -> dict:
    key = jax.random.key(seed)
    ks = iter(jax.random.split(key, 64))
    f32 = jnp.float32

    def nrm(shape, scale=1.0):
        return jax.random.normal(next(ks), shape, f32) * scale

    def gain(shape):
        return 1.0 + nrm(shape, 0.02)

    def unif(shape, lo, hi):
        return jax.random.uniform(next(ks), shape, f32, lo, hi)

    return {
        'x_prompt': nrm((BATCH, SEQ, D_MODEL)),
        'x_sample': nrm((DEC_BATCH, DEC_SEQ, D_MODEL)),
        'cache_k': nrm((DEC_BATCH, DEPTH, PAST_LEN, ATT_KV_HEADS, HEAD_DIM)),
        'cache_v': nrm((DEC_BATCH, DEPTH, PAST_LEN, ATT_KV_HEADS, HEAD_DIM)),
        'state_wkv_fwd': nrm((DEC_BATCH, DEPTH, RWKV_HEADS, RWKV_HEAD, RWKV_HEAD), 0.3),
        'state_wkv_bwd': nrm((DEC_BATCH, DEPTH, RWKV_HEADS, RWKV_HEAD, RWKV_HEAD), 0.3),
        'c': nrm((DEC_BATCH, D_MODEL)),
        'c_ctx': nrm((D_MODEL,)),
        'norm_mix': gain((DEPTH, D_MODEL)),
        'norm_ffn': gain((DEPTH, D_MODEL)),
        'w_mod': nrm((DEPTH, D_MODEL, 6 * D_MODEL), 0.5 * D_MODEL ** -0.5),
        'b_mod': nrm((DEPTH, 6 * D_MODEL), 0.02),
        'w_in': nrm((DEPTH, D_MODEL, IN_WIDTH), D_MODEL ** -0.5),
        'w_out': nrm((DEPTH, MIX_WIDTH, D_MODEL), MIX_WIDTH ** -0.5),
        'q_norm': gain((DEPTH, HEAD_DIM)),
        'k_norm': gain((DEPTH, HEAD_DIM)),
        'rwkv_mu': unif((DEPTH, RWKV_IN), 0.0, 1.0),
        'w0': unif((DEPTH, 2, RWKV_WIDTH), -6.0, -1.0),
        'w_up': nrm((DEPTH, 2, DECAY_LORA, RWKV_WIDTH), DECAY_LORA ** -0.5),
        'a0': nrm((DEPTH, 2, RWKV_WIDTH), 0.1),
        'a_up': nrm((DEPTH, 2, ICLR_LORA, RWKV_WIDTH), ICLR_LORA ** -0.5),
        'g_up': nrm((DEPTH, GATE_LORA, RWKV_WIDTH), GATE_LORA ** -0.5),
        'k_k': 0.85 + nrm((DEPTH, RWKV_WIDTH), 0.02),
        'k_a': gain((DEPTH, RWKV_WIDTH)),
        'r_k': nrm((DEPTH, RWKV_HEADS, RWKV_HEAD), 0.1),
        'lnx_w': gain((DEPTH, RWKV_WIDTH)),
        'lnx_b': nrm((DEPTH, RWKV_WIDTH), 0.02),
        'sgu_norm': gain((DEPTH, SGU_WIDTH)),
        'sgu_w': nrm((DEPTH, SGU_GROUPS, CHUNK, CHUNK), CHUNK ** -0.5),
        'sgu_b': nrm((DEPTH, SGU_GROUPS, CHUNK), 0.02),
        'w_router': nrm((DEPTH, D_MODEL, N_EXPERTS), D_MODEL ** -0.5),
        'b_router': nrm((DEPTH, N_EXPERTS), 0.01),
        'w_exp1': nrm((DEPTH, N_EXPERTS, D_MODEL, 2 * EXPERT_FF), D_MODEL ** -0.5),
        'b_exp1': nrm((DEPTH, N_EXPERTS, 2 * EXPERT_FF), 0.02),
        'w_exp2': nrm((DEPTH, N_EXPERTS, EXPERT_FF, D_MODEL), EXPERT_FF ** -0.5),
        'b_exp2': nrm((DEPTH, N_EXPERTS, D_MODEL), 0.02),
        'final_norm': gain((D_MODEL,)),
    }


def reference(x_prompt, x_sample, cache_k, cache_v, state_wkv_fwd, state_wkv_bwd, c, c_ctx,
              norm_mix, norm_ffn, w_mod, b_mod, w_in, w_out, q_norm, k_norm, rwkv_mu, w0, w_up, a0, a_up,
              g_up, k_k, k_a, r_k, lnx_w, lnx_b, sgu_norm, sgu_w, sgu_b, w_router, b_router,
              w_exp1, b_exp1, w_exp2, b_exp2, final_norm):
    rows = x_sample.shape[1] // GRID_W
    rope = _axial_rope(rows)
    experts = (w_exp1, b_exp1, w_exp2, b_exp2)
    x_ctx, x_lat = x_prompt, x_sample
    ks, vs, sfs, sbs = [], [], [], []
    for l in range(DEPTH):
        lp = {
            'norm_mix': norm_mix[l], 'norm_ffn': norm_ffn[l], 'w_mod': w_mod[l], 'b_mod': b_mod[l],
            'w_in': w_in[l], 'w_out': w_out[l], 'q_norm': q_norm[l], 'k_norm': k_norm[l],
            'rwkv_mu': rwkv_mu[l], 'w0': w0[l], 'w_up': w_up[l], 'a0': a0[l], 'a_up': a_up[l],
            'g_up': g_up[l], 'k_k': k_k[l], 'k_a': k_a[l], 'r_k': r_k[l], 'lnx_w': lnx_w[l],
            'lnx_b': lnx_b[l], 'sgu_norm': sgu_norm[l], 'sgu_w': sgu_w[l], 'sgu_b': sgu_b[l],
            'w_router': w_router[l], 'b_router': b_router[l],
        }
        x_ctx, k_l, v_l, sf_l, sb_l = _layer(x_ctx, c_ctx[None, :], lp, experts, l, None, None)
        ks.append(k_l)
        vs.append(v_l)
        sfs.append(sf_l)
        sbs.append(sb_l)
        ctx = (cache_k[:, l], cache_v[:, l], state_wkv_fwd[:, l], state_wkv_bwd[:, l])
        x_lat = _layer(x_lat, c, lp, experts, l, ctx, rope)[0]
    y_prompt = _rmsnorm(x_ctx, final_norm)
    y_sample = _rmsnorm(x_lat, final_norm)
    new_k = jnp.stack(ks, axis=1)
    new_v = jnp.stack(vs, axis=1)
    new_wkv_fwd = jnp.stack(sfs, axis=1).astype(x_prompt.dtype)
    new_wkv_bwd = jnp.stack(sbs, axis=1).astype(x_prompt.dtype)
    return (y_prompt, y_sample, new_k, new_v, new_wkv_fwd, new_wkv_bwd)
```

```python
import collections
import functools

import numpy as np
import jax
import jax.numpy as jnp
from jax import lax
from jax.experimental import pallas as pl
from jax.experimental.pallas import tpu as pltpu

F32 = jnp.float32
BF16 = jnp.bfloat16
HIGHEST = lax.Precision.HIGHEST

D_MODEL = 4096
DEPTH = 2
GRID_W = 64
HEAD_DIM = 128
ATT_WIDTH = 2048
ATT_KV_HEADS = 4
ATT_GROUP = 4
KV_WIDTH = 512
ROPE_THETA = 10000.0
ROPE_FREQS = 32
RWKV_WIDTH = 1024
RWKV_HEAD = 64
RWKV_PAIRS = 8
DECAY_LORA = 64
ICLR_LORA = 64
GATE_LORA = 160
RWKV_IN = 3 * RWKV_WIDTH + 2 * DECAY_LORA + 2 * ICLR_LORA + GATE_LORA
SGU_WIDTH = 1024
SGU_GROUPS = 8
CHUNK = 128
N_EXPERTS = 32
TOP_K = 4
EXPERT_FF = 2048
SWIGLU_LIMIT = 7.0
SWIGLU_ALPHA = 1.702
NORM_EPS = 1e-6
LNX_EPS = 64e-5
L2_EPS = 1e-12

LANES = 128
ROW_TILE = 256
SCAN_CHUNK = 64
VMEM_LIMIT = 48 * 1024 * 1024

COL_Q, COL_K, COL_V, COL_U, COL_SV, COL_R, COL_RK, COL_RV, COL_Z = 0, 2048, 2560, 3072, 4096, 5120, 6144, 7168, 8192
Z_WIDTH = 512
P_WIDTH = COL_Z + Z_WIDTH

MOE_TM = 256
MOE_TN1 = 256
MOE_TN2 = 512

Layout = collections.namedtuple("Layout", "n_ctx t_ctx n_lat t_lat past")


def _params(n_axes):
    return pltpu.CompilerParams(dimension_semantics=("arbitrary",) * n_axes, vmem_limit_bytes=VMEM_LIMIT)


def _rows(lay):
    return lay.n_ctx * lay.t_ctx + lay.n_lat * lay.t_lat


def _tile_mod_rows(lay, tile):
    n_ctx_rows = lay.n_ctx * lay.t_ctx
    starts = np.arange(0, _rows(lay), tile)
    return np.where(starts < n_ctx_rows, 0, 1 + (starts - n_ctx_rows) // lay.t_lat).astype(np.int32)


def _mod_kernel(c_ref, w_ref, b_ref, o_ref):
    c = c_ref[...]
    a = (c * jax.nn.sigmoid(c)).astype(BF16)
    o_ref[...] = jnp.dot(a, w_ref[...].astype(BF16), preferred_element_type=F32) + b_ref[...]


def _modulation(cond8, w_mod, b_mod, layer):
    tn = 512
    n = w_mod.shape[-1]
    return pl.pallas_call(
        _mod_kernel,
        out_shape=jax.ShapeDtypeStruct((8, n), F32),
        grid=(n // tn,),
        in_specs=[
            pl.BlockSpec((8, D_MODEL), lambda j: (0, 0)),
            pl.BlockSpec((None, D_MODEL, tn), lambda j: (layer, 0, j)),
            pl.BlockSpec((None, 1, tn), lambda j: (layer, 0, j)),
        ],
        out_specs=pl.BlockSpec((8, tn), lambda j: (0, j)),
        compiler_params=_params(1),
        name="modulation",
    )(cond8, w_mod, b_mod.reshape(DEPTH, 1, n))


def _modulated_norm(x_ref, g_ref, sh_ref, sc_ref):
    x = x_ref[...]
    y = x * lax.rsqrt(jnp.mean(x * x, axis=-1, keepdims=True) + NORM_EPS) * g_ref[...]
    return y * (1.0 + sc_ref[0]) + sh_ref[0]


def _norm_mod_kernel(tm_ref, x_ref, g_ref, sh_ref, sc_ref, o_ref):
    o_ref[...] = _modulated_norm(x_ref, g_ref, sh_ref, sc_ref).astype(o_ref.dtype)


def _norm_router_kernel(tm_ref, x_ref, g_ref, sh_ref, sc_ref, wr_ref, br_ref, o_ref, gate_ref, sel_ref):
    h = _modulated_norm(x_ref, g_ref, sh_ref, sc_ref)
    o_ref[...] = h.astype(o_ref.dtype)
    logits = jnp.dot(h, wr_ref[...], precision=HIGHEST, preferred_element_type=F32) + br_ref[...]
    lane = lax.broadcasted_iota(jnp.int32, logits.shape, 1)
    neg = jnp.float32(-jnp.inf)
    work = jnp.where(lane < N_EXPERTS, logits, neg)
    tops, hots = [], []
    for _ in range(TOP_K):
        m = jnp.max(work, axis=-1, keepdims=True)
        first = jnp.min(jnp.where(work == m, lane, LANES), axis=-1, keepdims=True)
        hot = lane == first
        tops.append(m)
        hots.append(hot)
        work = jnp.where(hot, neg, work)
    exps = [jnp.exp(t - tops[0]) for t in tops]
    denom = exps[0] + exps[1] + exps[2] + exps[3]
    gates = jnp.zeros_like(logits)
    sel = jnp.zeros_like(logits)
    for e, hot in zip(exps, hots):
        gates = jnp.where(hot, e / denom, gates)
        sel = jnp.where(hot, 1.0, sel)
    gate_ref[...] = gates
    sel_ref[...] = sel


def _norm_mod(x, gain, mod48, which_shift, lay, router=None):
    n = x.shape[0]
    tile_mod = jnp.asarray(_tile_mod_rows(lay, ROW_TILE))
    in_specs = [
        pl.BlockSpec((ROW_TILE, D_MODEL), lambda i, tm: (i, 0)),
        pl.BlockSpec((1, D_MODEL), lambda i, tm: (0, 0)),
        pl.BlockSpec((1, 1, D_MODEL), lambda i, tm: (tm[i] * 6 + which_shift, 0, 0)),
        pl.BlockSpec((1, 1, D_MODEL), lambda i, tm: (tm[i] * 6 + which_shift + 1, 0, 0)),
    ]
    h_spec = pl.BlockSpec((ROW_TILE, D_MODEL), lambda i, tm: (i, 0))
    if router is None:
        return pl.pallas_call(
            _norm_mod_kernel,
            out_shape=jax.ShapeDtypeStruct((n, D_MODEL), BF16),
            grid_spec=pltpu.PrefetchScalarGridSpec(
                num_scalar_prefetch=1, grid=(n // ROW_TILE,), in_specs=in_specs, out_specs=h_spec),
            compiler_params=_params(1),
            name="norm_mod",
        )(tile_mod, x, gain, mod48, mod48)
    wr, br = router
    lane_spec = pl.BlockSpec((ROW_TILE, LANES), lambda i, tm: (i, 0))
    return pl.pallas_call(
        _norm_router_kernel,
        out_shape=(jax.ShapeDtypeStruct((n, D_MODEL), BF16),
                   jax.ShapeDtypeStruct((n, LANES), F32),
                   jax.ShapeDtypeStruct((n, LANES), F32)),
        grid_spec=pltpu.PrefetchScalarGridSpec(
            num_scalar_prefetch=1, grid=(n // ROW_TILE,),
            in_specs=in_specs + [pl.BlockSpec((D_MODEL, LANES), lambda i, tm: (0, 0)),
                                 pl.BlockSpec((1, LANES), lambda i, tm: (0, 0))],
            out_specs=(h_spec, lane_spec, lane_spec)),
        compiler_params=_params(1),
        name="norm_router",
    )(tile_mod, x, gain, mod48, mod48, wr, br)


def _mm_kernel(a_ref, w_ref, o_ref):
    o_ref[...] = jnp.dot(a_ref[...], w_ref[...], preferred_element_type=F32)


def _matmul(a, w, tm, tn):
    m, k = a.shape
    n = w.shape[1]
    return pl.pallas_call(
        _mm_kernel,
        out_shape=jax.ShapeDtypeStruct((m, n), F32),
        grid=(m // tm, n // tn),
        in_specs=[pl.BlockSpec((tm, k), lambda i, j: (i, 0)),
                  pl.BlockSpec((k, tn), lambda i, j: (0, j))],
        out_specs=pl.BlockSpec((tm, tn), lambda i, j: (i, j)),
        compiler_params=_params(2),
        name="in_proj",
    )(a, w)


def _head_norm(x, g):
    return x * lax.rsqrt(jnp.mean(x * x, axis=-1, keepdims=True) + NORM_EPS) * g


def _rope(x, cos, sin_signed):
    lane = lax.broadcasted_iota(jnp.int32, x.shape, 1)
    partner = jnp.where((lane & 32) == 0, pltpu.roll(x, LANES - 32, 1), pltpu.roll(x, 32, 1))
    return x * cos + partner * sin_signed


def _softmax_pv(q_bf, k_bf, v_bf):
    s = lax.dot_general(q_bf, k_bf, (((1,), (1,)), ((), ())), preferred_element_type=F32) * (HEAD_DIM ** -0.5)
    e = jnp.exp(s - jnp.max(s, axis=-1, keepdims=True))
    o = jnp.dot(e.astype(BF16), v_bf, preferred_element_type=F32)
    return o / jnp.sum(e, axis=-1, keepdims=True)


def _attn_ctx_kernel(q_ref, k_ref, v_ref, qn_ref, kn_ref, o_ref, ko_ref):
    t = q_ref.shape[0]
    kn = _head_norm(k_ref[...], kn_ref[...])
    ko_ref[...] = kn
    qs = [_head_norm(q_ref[:, h * HEAD_DIM:(h + 1) * HEAD_DIM], qn_ref[...]) for h in range(ATT_GROUP)]
    q = jnp.concatenate(qs, axis=0).astype(BF16)
    o = _softmax_pv(q, kn.astype(BF16), v_ref[...].astype(BF16))
    for h in range(ATT_GROUP):
        o_ref[:, h * HEAD_DIM:(h + 1) * HEAD_DIM] = o[h * t:(h + 1) * t].astype(o_ref.dtype)


def _attention_ctx(proj, qn, kn, lay):
    t = lay.t_ctx
    rows = lay.n_ctx * t
    kcol, vcol = COL_K // HEAD_DIM, COL_V // HEAD_DIM
    return pl.pallas_call(
        _attn_ctx_kernel,
        out_shape=(jax.ShapeDtypeStruct((rows, ATT_WIDTH), BF16), jax.ShapeDtypeStruct((rows, KV_WIDTH), F32)),
        grid=(lay.n_ctx, ATT_KV_HEADS),
        in_specs=[pl.BlockSpec((t, ATT_GROUP * HEAD_DIM), lambda b, g: (b, g)),
                  pl.BlockSpec((t, HEAD_DIM), lambda b, g: (b, kcol + g)),
                  pl.BlockSpec((t, HEAD_DIM), lambda b, g: (b, vcol + g)),
                  pl.BlockSpec((1, HEAD_DIM), lambda b, g: (0, 0)),
                  pl.BlockSpec((1, HEAD_DIM), lambda b, g: (0, 0))],
        out_specs=(pl.BlockSpec((t, ATT_GROUP * HEAD_DIM), lambda b, g: (b, g)),
                   pl.BlockSpec((t, HEAD_DIM), lambda b, g: (b, g))),
        compiler_params=_params(2),
        name="attention_ctx",
    )(proj, proj, proj, qn, kn)


def _attn_lat_kernel(q_ref, k_ref, v_ref, ck_ref, cv_ref, cosq_ref, sinq_ref, cosk_ref, sink_ref,
                     qn_ref, kn_ref, o_ref, kbuf, vbuf):
    past = ck_ref.shape[0]
    tq = q_ref.shape[0]

    @pl.when(pl.program_id(2) == 0)
    def _():
        kr = _rope(_head_norm(k_ref[...], kn_ref[...]), cosk_ref[...], sink_ref[...])
        kbuf[0:past, :] = ck_ref[...].astype(BF16)
        kbuf[past:, :] = kr.astype(BF16)
        vbuf[0:past, :] = cv_ref[...].astype(BF16)
        vbuf[past:, :] = v_ref[...].astype(BF16)

    cos, sin = cosq_ref[...], sinq_ref[...]
    qs = [_rope(_head_norm(q_ref[:, h * HEAD_DIM:(h + 1) * HEAD_DIM], qn_ref[...]), cos, sin)
          for h in range(ATT_GROUP)]
    q = jnp.concatenate(qs, axis=0).astype(BF16)
    o = _softmax_pv(q, kbuf[...], vbuf[...])
    for h in range(ATT_GROUP):
        o_ref[:, h * HEAD_DIM:(h + 1) * HEAD_DIM] = o[h * tq:(h + 1) * tq].astype(o_ref.dtype)


def _attention_lat(proj, cache_k, cache_v, cos, sin, qn, kn, layer, lay):
    t, past = lay.t_lat, lay.past
    tq = ROW_TILE
    nq = t // tq
    base_q = lay.n_ctx * lay.t_ctx // tq
    base_t = lay.n_ctx * lay.t_ctx // t
    kcol, vcol = COL_K // HEAD_DIM, COL_V // HEAD_DIM
    cache_spec = pl.BlockSpec((None, None, past, HEAD_DIM), lambda b, g, i: (b, layer, 0, g))
    return pl.pallas_call(
        _attn_lat_kernel,
        out_shape=jax.ShapeDtypeStruct((lay.n_lat * t, ATT_WIDTH), BF16),
        grid=(lay.n_lat, ATT_KV_HEADS, nq),
        in_specs=[pl.BlockSpec((tq, ATT_GROUP * HEAD_DIM), lambda b, g, i: (base_q + b * nq + i, g)),
                  pl.BlockSpec((t, HEAD_DIM), lambda b, g, i: (base_t + b, kcol + g)),
                  pl.BlockSpec((t, HEAD_DIM), lambda b, g, i: (base_t + b, vcol + g)),
                  cache_spec, cache_spec,
                  pl.BlockSpec((tq, HEAD_DIM), lambda b, g, i: (i, 0)),
                  pl.BlockSpec((tq, HEAD_DIM), lambda b, g, i: (i, 0)),
                  pl.BlockSpec((t, HEAD_DIM), lambda b, g, i: (0, 0)),
                  pl.BlockSpec((t, HEAD_DIM), lambda b, g, i: (0, 0)),
                  pl.BlockSpec((1, HEAD_DIM), lambda b, g, i: (0, 0)),
                  pl.BlockSpec((1, HEAD_DIM), lambda b, g, i: (0, 0))],
        out_specs=pl.BlockSpec((tq, ATT_GROUP * HEAD_DIM), lambda b, g, i: (b * nq + i, g)),
        scratch_shapes=[pltpu.VMEM((past + t, HEAD_DIM), BF16), pltpu.VMEM((past + t, HEAD_DIM), BF16)],
        compiler_params=_params(3),
        name="attention_lat",
    )(proj, proj, proj, cache_k, cache_v, cos, sin, cos, sin, qn, kn)


def _rope_tables(t):
    pos = np.arange(t)
    inv = ROPE_THETA ** (-np.arange(ROPE_FREQS, dtype=np.float32) / ROPE_FREQS)
    ang_r = (pos // GRID_W).astype(np.float32)[:, None] * inv
    ang_c = (pos % GRID_W).astype(np.float32)[:, None] * inv
    cr, sr, cc, sc = np.cos(ang_r), np.sin(ang_r), np.cos(ang_c), np.sin(ang_c)
    cos = np.concatenate([cr, cr, cc, cc], axis=1)
    sin = np.concatenate([-sr, sr, -sc, sc], axis=1)
    return jnp.asarray(cos, F32), jnp.asarray(sin, F32)


def _sgu_kernel(u_ref, sv_ref, w_ref, bt_ref, g_ref, o_ref):
    for g in range(SGU_GROUPS):
        cols = slice(g * LANES, (g + 1) * LANES)
        sv = sv_ref[:, cols]
        svn = sv * lax.rsqrt(jnp.mean(sv * sv, axis=-1, keepdims=True) + NORM_EPS) * g_ref[:, cols]
        z = jnp.dot(w_ref[g], svn.astype(BF16), preferred_element_type=F32) + bt_ref[:, g:g + 1]
        o_ref[:, cols] = (u_ref[:, cols] * z).astype(o_ref.dtype)


def _sgu(proj, w_bf, b_t, gain):
    n = proj.shape[0]
    ucol, svcol = COL_U // SGU_WIDTH, COL_SV // SGU_WIDTH
    return pl.pallas_call(
        _sgu_kernel,
        out_shape=jax.ShapeDtypeStruct((n, SGU_WIDTH), BF16),
        grid=(n // CHUNK,),
        in_specs=[pl.BlockSpec((CHUNK, SGU_WIDTH), lambda i: (i, ucol)),
                  pl.BlockSpec((CHUNK, SGU_WIDTH), lambda i: (i, svcol)),
                  pl.BlockSpec((SGU_GROUPS, CHUNK, CHUNK), lambda i: (0, 0, 0)),
                  pl.BlockSpec((CHUNK, SGU_GROUPS), lambda i: (0, 0)),
                  pl.BlockSpec((1, SGU_WIDTH), lambda i: (0, 0))],
        out_specs=pl.BlockSpec((CHUNK, SGU_WIDTH), lambda i: (i, 0)),
        compiler_params=_params(1),
        name="sgu",
    )(proj, proj, w_bf, b_t, gain)


def _pair_sum(x, ones_bd):
    blocks = [jnp.dot(x[:, p * LANES:(p + 1) * LANES], ones_bd, precision=HIGHEST, preferred_element_type=F32)
              for p in range(x.shape[1] // LANES)]
    return jnp.concatenate(blocks, axis=1)


def _rwkv_prep_kernel(first_ref, last_ref,
                      r_ref, k_ref, v_ref, z_ref, rp_ref, kp_ref, vp_ref, zp_ref, rn_ref, kn_ref, vn_ref, zn_ref,
                      mur_ref, muk_ref, muv_ref, muz_ref, kk_ref, ka_ref, rk_ref, w0_ref, a0_ref,
                      wup_ref, aup_ref, gup_ref, ones_ref,
                      r_o, v_o, kk_o, lwf_o, lwb_o, bf_o, bb_o, kdf_o, kdb_o, bonus_o, gate_o):
    i = pl.program_id(0)
    has_prev = first_ref[i] == 0
    has_next = last_ref[i] == 0
    row = lax.broadcasted_iota(jnp.int32, (ROW_TILE, 1), 0)

    def mix(x_ref, p_ref, n_ref, mu_ref):
        x = x_ref[...]
        prev = jnp.where(row == 0, jnp.where(has_prev, p_ref[7:8, :], 0.0), pltpu.roll(x, 1, 0))
        nxt = jnp.where(row == ROW_TILE - 1, jnp.where(has_next, n_ref[0:1, :], 0.0),
                        pltpu.roll(x, ROW_TILE - 1, 0))
        return x + (0.5 * (prev + nxt) - x) * mu_ref[...]

    r = mix(r_ref, rp_ref, rn_ref, mur_ref)
    k = mix(k_ref, kp_ref, kn_ref, muk_ref)
    v = mix(v_ref, vp_ref, vn_ref, muv_ref)
    z = mix(z_ref, zp_ref, zn_ref, muz_ref)
    ones_bd = ones_ref[...]

    kk = k * kk_ref[...]
    kk = kk / jnp.maximum(jnp.sqrt(_pair_sum(kk * kk, ones_bd)), L2_EPS)

    lora_w = jnp.dot(jnp.tanh(z[:, 0:LANES]).astype(BF16), wup_ref[...], preferred_element_type=F32)
    lora_a = jnp.dot(z[:, LANES:2 * LANES].astype(BF16), aup_ref[...], preferred_element_type=F32)
    gate_o[...] = jnp.dot(jax.nn.sigmoid(z[:, 2 * LANES:4 * LANES]).astype(BF16), gup_ref[...],
                          preferred_element_type=F32)

    iclr_sum = jnp.zeros_like(k)
    for d, (lw_o, b_o, kd_o) in enumerate(((lwf_o, bf_o, kdf_o), (lwb_o, bb_o, kdb_o))):
        cols = slice(d * RWKV_WIDTH, (d + 1) * RWKV_WIDTH)
        x = -(w0_ref[d:d + 1, :] + lora_w[:, cols])
        softplus = jnp.maximum(x, 0.0) + jnp.log(1.0 + jnp.exp(-jnp.abs(x)))
        log_decay = -jnp.exp(-softplus - 0.5)
        iclr = jax.nn.sigmoid(a0_ref[d:d + 1, :] + lora_a[:, cols])
        kd = k * (1.0 + (iclr - 1.0) * ka_ref[...])
        b = kk * iclr
        iclr_sum = iclr_sum + iclr
        for p in range(RWKV_PAIRS):
            pc = slice(p * LANES, (p + 1) * LANES)
            lw_o[p] = log_decay[:, pc]
            b_o[p] = b[:, pc]
            kd_o[p] = kd[:, pc]
    for p in range(RWKV_PAIRS):
        pc = slice(p * LANES, (p + 1) * LANES)
        r_o[p] = r[:, pc]
        v_o[p] = v[:, pc]
        kk_o[p] = kk[:, pc]
    bonus_o[...] = _pair_sum(r * k * rk_ref[...] * (2.0 + (iclr_sum - 2.0) * ka_ref[...]), ones_bd) * v


def _seq_edge_flags(lay):
    starts = np.arange(0, _rows(lay), ROW_TILE)
    n_ctx_rows = lay.n_ctx * lay.t_ctx
    in_seq = np.where(starts < n_ctx_rows, starts % lay.t_ctx, (starts - n_ctx_rows) % lay.t_lat)
    seq_len = np.where(starts < n_ctx_rows, lay.t_ctx, lay.t_lat)
    first = (in_seq == 0).astype(np.int32)
    last = (in_seq + ROW_TILE == seq_len).astype(np.int32)
    return jnp.asarray(first), jnp.asarray(last)


def _rwkv_prep(proj, rp, lay):
    n = proj.shape[0]
    n_tiles = n // ROW_TILE
    first, last = _seq_edge_flags(lay)
    sub = ROW_TILE // 8
    wcols = {"r": (RWKV_WIDTH, COL_R // RWKV_WIDTH), "k": (RWKV_WIDTH, COL_RK // RWKV_WIDTH),
             "v": (RWKV_WIDTH, COL_RV // RWKV_WIDTH), "z": (Z_WIDTH, COL_Z // Z_WIDTH)}
    main = [pl.BlockSpec((ROW_TILE, w), functools.partial(lambda i, f, l, c: (i, c), c=c))
            for w, c in wcols.values()]
    prev = [pl.BlockSpec((8, w), functools.partial(lambda i, f, l, c: (jnp.maximum(i * sub - 1, 0), c), c=c))
            for w, c in wcols.values()]
    nxt = [pl.BlockSpec((8, w), functools.partial(
        lambda i, f, l, c: (jnp.minimum((i + 1) * sub, n_tiles * sub - 1), c), c=c)) for w, c in wcols.values()]

    def const(shape):
        return pl.BlockSpec(shape, lambda i, f, l: (0,) * len(shape))

    consts = [const((1, RWKV_WIDTH))] * 3 + [const((1, Z_WIDTH))] + [const((1, RWKV_WIDTH))] * 3 + \
        [const((2, RWKV_WIDTH))] * 2 + [const((LANES, 2 * RWKV_WIDTH))] * 2 + \
        [const((2 * LANES, RWKV_WIDTH)), const((LANES, LANES))]
    pair_out = jax.ShapeDtypeStruct((RWKV_PAIRS, n, LANES), F32)
    pair_spec = pl.BlockSpec((RWKV_PAIRS, ROW_TILE, LANES), lambda i, f, l: (0, i, 0))
    row_out = jax.ShapeDtypeStruct((n, RWKV_WIDTH), F32)
    row_spec = pl.BlockSpec((ROW_TILE, RWKV_WIDTH), lambda i, f, l: (i, 0))
    return pl.pallas_call(
        _rwkv_prep_kernel,
        out_shape=(pair_out,) * 9 + (row_out,) * 2,
        grid_spec=pltpu.PrefetchScalarGridSpec(
            num_scalar_prefetch=2, grid=(n_tiles,),
            in_specs=main + prev + nxt + consts,
            out_specs=(pair_spec,) * 9 + (row_spec,) * 2),
        compiler_params=_params(1),
        name="rwkv_prep",
    )(first, last, *([proj] * 12), rp["mu_r"], rp["mu_k"], rp["mu_v"], rp["mu_z"], rp["k_k"], rp["k_a"],
      rp["r_k"], rp["w0"], rp["a0"], rp["w_up"], rp["a_up"], rp["g_up"], rp["ones_bd"])


def _bf(x):
    return x.astype(BF16)


def _mm(a, b):
    return jnp.dot(_bf(a), _bf(b), preferred_element_type=F32)


def _mm_nt(a, b):
    return lax.dot_general(_bf(a), _bf(b), (((1,), (1,)), ((), ())), preferred_element_type=F32)


def _mm_tn(a, b):
    return jnp.dot(_bf(a.T), _bf(b), preferred_element_type=F32)


def _scan_block(h, r, v, kk, lw, b, kd, reverse):
    c = SCAN_CHUNK
    two = 2 * c
    t_i = lax.broadcasted_iota(jnp.int32, (c, c), 0)
    s_i = lax.broadcasted_iota(jnp.int32, (c, c), 1)
    tri = jnp.where((s_i >= t_i) if reverse else (s_i <= t_i), 1.0, 0.0)
    cum = jnp.dot(tri, lw, precision=HIGHEST, preferred_element_type=F32)
    total = cum[0:1, :] if reverse else cum[c - 1:c, :]
    total_rows = jnp.dot(lw.T, jnp.ones((c, LANES), F32), precision=HIGHEST, preferred_element_type=F32)
    g_incl, g_inv, g_end = jnp.exp(cum), jnp.exp(-cum), jnp.exp(total - cum)
    g_excl = jnp.exp(cum - lw)

    lane = lax.broadcasted_iota(jnp.int32, (c, LANES), 1)
    head0 = lane < RWKV_HEAD

    def stack(x):
        return jnp.concatenate([jnp.where(head0, x, 0.0), jnp.where(head0, 0.0, x)], axis=0)

    a_s, r_s = stack(-kk * g_excl), stack(r * g_incl)
    b_s, k_s = stack(b * g_inv), stack(kd * g_inv)
    be_s, ke_s = stack(b * g_end), stack(kd * g_end)
    v_s = stack(v)

    row2 = lax.broadcasted_iota(jnp.int32, (two, two), 0)
    col2 = lax.broadcasted_iota(jnp.int32, (two, two), 1)
    same_head = (row2 < c) == (col2 < c)
    before = (col2 > row2) if reverse else (col2 < row2)
    strict = same_head & before
    incl = same_head & (before | (row2 == col2))
    a_ab = jnp.where(strict, _mm_nt(a_s, b_s), 0.0)
    a_ak = jnp.where(strict, _mm_nt(a_s, k_s), 0.0)
    a_rb = jnp.where(incl, _mm_nt(r_s, b_s), 0.0)
    a_rk = jnp.where(incl, _mm_nt(r_s, k_s), 0.0)

    eye = jnp.where(row2 == col2, 1.0, 0.0)
    inv = eye + a_ab
    power = a_ab
    for _ in range(int(np.log2(c)) - 1):
        power = _mm(power, power)
        inv = inv + _mm(power, inv)

    p = _mm(inv, a_s)
    q = _mm(inv, _mm(a_ak, v_s))
    qv = jnp.concatenate([q, v_s], axis=0)
    g_mat = r_s + _mm(a_rb, p)
    y0 = _mm(jnp.concatenate([a_rb, a_rk], axis=1), qv)
    m_low = _mm_tn(be_s, p)
    n_mat = _mm_tn(jnp.concatenate([be_s, ke_s], axis=0), qv)

    y = _mm(g_mat, h) + y0
    h_new = jnp.exp(total_rows) * h + _mm(m_low, h) + n_mat
    return h_new, y[0:c, :] + y[c:two, :]


def _rwkv_scan_kernel(r_ref, v_ref, kk_ref, lwf_ref, lwb_ref, bf_ref, bb_ref, kdf_ref, kdb_ref,
                      bonus_ref, gate_ref, lnw_ref, lnb_ref, ones_ref, hf0_ref, hb0_ref,
                      o_ref, hf_o, hb_o, yf, yb):
    t = r_ref.shape[0]
    nc = t // SCAN_CHUNK
    hf_o[...] = hf0_ref[...]
    hb_o[...] = hb0_ref[...]

    def body(ci, carry):
        for reverse, lw_ref, b_ref, kd_ref, h_ref, y_ref in ((False, lwf_ref, bf_ref, kdf_ref, hf_o, yf),
                                                            (True, lwb_ref, bb_ref, kdb_ref, hb_o, yb)):
            cc = (nc - 1 - ci) if reverse else ci
            rows = pl.ds(pl.multiple_of(cc * SCAN_CHUNK, SCAN_CHUNK), SCAN_CHUNK)
            h_new, y = _scan_block(h_ref[...], r_ref[rows, :], v_ref[rows, :], kk_ref[rows, :],
                                   lw_ref[rows, :], b_ref[rows, :], kd_ref[rows, :], reverse)
            h_ref[...] = h_new
            y_ref[rows, :] = y
        return carry

    lax.fori_loop(0, nc, body, 0)

    ones_bd = ones_ref[...]
    y = yf[...] + yb[...]
    mean = jnp.dot(y, ones_bd, precision=HIGHEST, preferred_element_type=F32) * (1.0 / RWKV_HEAD)
    dev = y - mean
    var = jnp.dot(dev * dev, ones_bd, precision=HIGHEST, preferred_element_type=F32) * (1.0 / RWKV_HEAD)
    yn = dev * lax.rsqrt(var + LNX_EPS) * lnw_ref[...] + lnb_ref[...]
    o_ref[...] = ((yn + bonus_ref[...]) * gate_ref[...]).astype(o_ref.dtype)


def _rwkv_scan(prep, lnx_w, lnx_b, ones_bd, h0f, h0b, n_seq, t, row_base):
    r, v, kk, lwf, lwb, bf, bb, kdf, kdb, bonus, gate = prep
    base = row_base // t
    pair_spec = pl.BlockSpec((None, t, LANES), lambda s, p: (p, base + s, 0))
    row_spec = pl.BlockSpec((t, LANES), lambda s, p: (base + s, p))
    vec_spec = pl.BlockSpec((1, LANES), lambda s, p: (0, p))
    h_spec = pl.BlockSpec((None, None, LANES, LANES), lambda s, p: (s, p, 0, 0))
    h_shape = jax.ShapeDtypeStruct((n_seq, RWKV_PAIRS, LANES, LANES), F32)
    return pl.pallas_call(
        _rwkv_scan_kernel,
        out_shape=(jax.ShapeDtypeStruct((n_seq * t, RWKV_WIDTH), BF16), h_shape, h_shape),
        grid=(n_seq, RWKV_PAIRS),
        in_specs=[pair_spec] * 9 + [row_spec, row_spec, vec_spec, vec_spec,
                                    pl.BlockSpec((LANES, LANES), lambda s, p: (0, 0)), h_spec, h_spec],
        out_specs=(pl.BlockSpec((t, LANES), lambda s, p: (s, p)), h_spec, h_spec),
        scratch_shapes=[pltpu.VMEM((t, LANES), F32), pltpu.VMEM((t, LANES), F32)],
        compiler_params=_params(2),
        name="rwkv_scan",
    )(r, v, kk, lwf, lwb, bf, bb, kdf, kdb, bonus, gate, lnx_w, lnx_b, ones_bd, h0f, h0b)


def _state_to_blockdiag(s):
    n = s.shape[0]
    h = jnp.swapaxes(s, -1, -2).reshape(n, RWKV_PAIRS, 2, RWKV_HEAD, RWKV_HEAD)
    z = jnp.zeros_like(h[:, :, 0])
    top = jnp.concatenate([h[:, :, 0], z], axis=-1)
    bot = jnp.concatenate([z, h[:, :, 1]], axis=-1)
    return jnp.concatenate([top, bot], axis=-2)


def _blockdiag_to_state(h):
    n = h.shape[0]
    h0 = h[:, :, :RWKV_HEAD, :RWKV_HEAD]
    h1 = h[:, :, RWKV_HEAD:, RWKV_HEAD:]
    s = jnp.stack([h0, h1], axis=2).reshape(n, 2 * RWKV_PAIRS, RWKV_HEAD, RWKV_HEAD)
    return jnp.swapaxes(s, -1, -2)


def _out_proj_kernel(tm_ref, att_ref, rw_ref, sg_ref, w_ref, x_ref, g_ref, o_ref):
    acc = jnp.dot(att_ref[...], w_ref[0:ATT_WIDTH, :], preferred_element_type=F32)
    acc += jnp.dot(rw_ref[...], w_ref[ATT_WIDTH:ATT_WIDTH + RWKV_WIDTH, :], preferred_element_type=F32)
    acc += jnp.dot(sg_ref[...], w_ref[ATT_WIDTH + RWKV_WIDTH:, :], preferred_element_type=F32)
    o_ref[...] = x_ref[...] + g_ref[0] * acc


def _out_proj(att, rw, sg, w_bf, x, mod48, lay):
    n = x.shape[0]
    tm, tn = 512, 512
    tile_mod = jnp.asarray(_tile_mod_rows(lay, tm))
    return pl.pallas_call(
        _out_proj_kernel,
        out_shape=jax.ShapeDtypeStruct((n, D_MODEL), F32),
        grid_spec=pltpu.PrefetchScalarGridSpec(
            num_scalar_prefetch=1, grid=(n // tm, D_MODEL // tn),
            in_specs=[pl.BlockSpec((tm, ATT_WIDTH), lambda i, j, t: (i, 0)),
                      pl.BlockSpec((tm, RWKV_WIDTH), lambda i, j, t: (i, 0)),
                      pl.BlockSpec((tm, SGU_WIDTH), lambda i, j, t: (i, 0)),
                      pl.BlockSpec((D_MODEL, tn), lambda i, j, t: (0, j)),
                      pl.BlockSpec((tm, tn), lambda i, j, t: (i, j)),
                      pl.BlockSpec((1, 1, tn), lambda i, j, t: (t[i] * 6 + 2, 0, j))],
            out_specs=pl.BlockSpec((tm, tn), lambda i, j, t: (i, j))),
        compiler_params=_params(2),
        name="out_proj",
    )(tile_mod, att, rw, sg, w_bf, x, mod48)


def _moe_up_kernel(te_ref, tx_ref, tv_ref, tf_ref, x_ref, wg_ref, wl_ref, bg_ref, bl_ref, o_ref, wg_bf, wl_bf):
    i = pl.program_id(1)

    @pl.when(tf_ref[i] == 1)
    def _():
        wg_bf[...] = wg_ref[...].astype(BF16)
        wl_bf[...] = wl_ref[...].astype(BF16)

    @pl.when(tv_ref[i] == 1)
    def _():
        x = x_ref[...]
        z_glu = jnp.dot(x, wg_bf[...], preferred_element_type=F32) + bg_ref[...]
        z_lin = jnp.dot(x, wl_bf[...], preferred_element_type=F32) + bl_ref[...]
        z_glu = jnp.minimum(z_glu, SWIGLU_LIMIT)
        z_lin = jnp.clip(z_lin, -SWIGLU_LIMIT, SWIGLU_LIMIT)
        o_ref[...] = (z_glu * jax.nn.sigmoid(SWIGLU_ALPHA * z_glu) * (z_lin + 1.0)).astype(o_ref.dtype)


def _moe_down_kernel(te_ref, tx_ref, tv_ref, tf_ref, a_ref, w_ref, b_ref, gate_ref, o_ref, w_bf):
    i = pl.program_id(1)

    @pl.when(tf_ref[i] == 1)
    def _():
        w_bf[...] = w_ref[...].astype(BF16)

    @pl.when(tv_ref[i] == 1)
    def _():
        y = jnp.dot(a_ref[...], w_bf[...], preferred_element_type=F32) + b_ref[...]
        o_ref[...] = y * gate_ref[...]


def _moe_experts(xs, gate_sorted, tiles, w1, b1, w2, b2, layer):
    te, tx, tv, tf = tiles
    r = xs.shape[0]
    n_tiles = r // MOE_TM
    nj1 = EXPERT_FF // MOE_TN1
    act = pl.pallas_call(
        _moe_up_kernel,
        out_shape=jax.ShapeDtypeStruct((r, EXPERT_FF), BF16),
        grid_spec=pltpu.PrefetchScalarGridSpec(
            num_scalar_prefetch=4, grid=(nj1, n_tiles),
            in_specs=[pl.BlockSpec((MOE_TM, D_MODEL), lambda j, i, te, tx, tv, tf: (tx[i], 0)),
                      pl.BlockSpec((None, None, D_MODEL, MOE_TN1), lambda j, i, te, tx, tv, tf: (layer, te[i], 0, j)),
                      pl.BlockSpec((None, None, D_MODEL, MOE_TN1),
                                   lambda j, i, te, tx, tv, tf: (layer, te[i], 0, nj1 + j)),
                      pl.BlockSpec((None, 1, MOE_TN1), lambda j, i, te, tx, tv, tf: (layer * N_EXPERTS + te[i], 0, j)),
                      pl.BlockSpec((None, 1, MOE_TN1),
                                   lambda j, i, te, tx, tv, tf: (layer * N_EXPERTS + te[i], 0, nj1 + j))],
            out_specs=pl.BlockSpec((MOE_TM, MOE_TN1), lambda j, i, te, tx, tv, tf: (tx[i], j)),
            scratch_shapes=[pltpu.VMEM((D_MODEL, MOE_TN1), BF16), pltpu.VMEM((D_MODEL, MOE_TN1), BF16)]),
        compiler_params=_params(2),
        name="moe_up",
    )(te, tx, tv, tf, xs, w1, w1, b1, b1)
    nj2 = D_MODEL // MOE_TN2
    return pl.pallas_call(
        _moe_down_kernel,
        out_shape=jax.ShapeDtypeStruct((r, D_MODEL), F32),
        grid_spec=pltpu.PrefetchScalarGridSpec(
            num_scalar_prefetch=4, grid=(nj2, n_tiles),
            in_specs=[pl.BlockSpec((MOE_TM, EXPERT_FF), lambda j, i, te, tx, tv, tf: (tx[i], 0)),
                      pl.BlockSpec((None, None, EXPERT_FF, MOE_TN2), lambda j, i, te, tx, tv, tf: (layer, te[i], 0, j)),
                      pl.BlockSpec((None, 1, MOE_TN2), lambda j, i, te, tx, tv, tf: (layer * N_EXPERTS + te[i], 0, j)),
                      pl.BlockSpec((MOE_TM, 1), lambda j, i, te, tx, tv, tf: (tx[i], 0))],
            out_specs=pl.BlockSpec((MOE_TM, MOE_TN2), lambda j, i, te, tx, tv, tf: (tx[i], j)),
            scratch_shapes=[pltpu.VMEM((EXPERT_FF, MOE_TN2), BF16)]),
        compiler_params=_params(2),
        name="moe_down",
    )(te, tx, tv, tf, act, w2, b2, gate_sorted)


def _route(sel, gates):
    n = sel.shape[0]
    r_max = n * TOP_K + N_EXPERTS * MOE_TM
    n_tiles = r_max // MOE_TM
    sel_i = sel.astype(jnp.int32)
    counts = jnp.sum(sel_i, axis=0)
    tiles_per = (counts + MOE_TM - 1) // MOE_TM
    tile_end = jnp.cumsum(tiles_per)
    offsets = (tile_end - tiles_per) * MOE_TM
    rank = jnp.cumsum(sel_i, axis=0) - sel_i
    pos = jnp.where(sel_i > 0, offsets[None, :] + rank, r_max)
    tok = jnp.broadcast_to(jnp.arange(n, dtype=jnp.int32)[:, None], pos.shape)
    src = jnp.zeros((r_max,), jnp.int32).at[pos.reshape(-1)].set(tok.reshape(-1), mode="drop")
    gate_sorted = jnp.zeros((r_max,), F32).at[pos.reshape(-1)].set(gates.reshape(-1), mode="drop")
    pos4 = jnp.sort(pos, axis=1)[:, :TOP_K]
    n_used = tile_end[-1]
    tile_ids = jnp.arange(n_tiles, dtype=jnp.int32)
    tx = jnp.minimum(tile_ids, n_used - 1)
    te = jnp.minimum(jnp.searchsorted(tile_end, tx, side="right"), N_EXPERTS - 1).astype(jnp.int32)
    tv = (tile_ids < n_used).astype(jnp.int32)
    tf = jnp.concatenate([jnp.ones((1,), jnp.int32), (te[1:] != te[:-1]).astype(jnp.int32)])
    return src, gate_sorted.reshape(r_max, 1), pos4, (te, tx.astype(jnp.int32), tv, tf)


def _resid_kernel(tm_ref, x_ref, y_ref, g_ref, o_ref):
    o_ref[...] = x_ref[...] + g_ref[0] * y_ref[...]


def _resid_norm_kernel(tm_ref, x_ref, y_ref, g_ref, fn_ref, o_ref):
    x = x_ref[...] + g_ref[0] * y_ref[...]
    o_ref[...] = x * lax.rsqrt(jnp.mean(x * x, axis=-1, keepdims=True) + NORM_EPS) * fn_ref[...]


def _gated_residual(x, y, mod48, lay, final_norm=None):
    n = x.shape[0]
    tile_mod = jnp.asarray(_tile_mod_rows(lay, ROW_TILE))
    tile = pl.BlockSpec((ROW_TILE, D_MODEL), lambda i, t: (i, 0))
    in_specs = [tile, tile, pl.BlockSpec((1, 1, D_MODEL), lambda i, t: (t[i] * 6 + 5, 0, 0))]
    args = [tile_mod, x, y, mod48]
    kern = _resid_kernel
    if final_norm is not None:
        in_specs.append(pl.BlockSpec((1, D_MODEL), lambda i, t: (0, 0)))
        args.append(final_norm)
        kern = _resid_norm_kernel
    return pl.pallas_call(
        kern,
        out_shape=jax.ShapeDtypeStruct((n, D_MODEL), F32),
        grid_spec=pltpu.PrefetchScalarGridSpec(
            num_scalar_prefetch=1, grid=(n // ROW_TILE,), in_specs=in_specs, out_specs=tile),
        compiler_params=_params(1),
        name="gated_residual",
    )(*args)


def _block_ones():
    idx = np.arange(LANES) // RWKV_HEAD
    return jnp.asarray((idx[:, None] == idx[None, :]).astype(np.float32))


def _layer_weights(l, w_in, w_out, rwkv_mu, w0, w_up, a0, a_up, g_up, k_k, k_a, r_k, sgu_w, sgu_b, w_router, b_router):
    pad = jnp.zeros((D_MODEL, P_WIDTH - w_in.shape[-1]), w_in.dtype)
    rw_lo, rw_hi = ATT_WIDTH + 2 * KV_WIDTH, ATT_WIDTH + 2 * KV_WIDTH + RWKV_IN
    w_cat = jnp.concatenate([w_in[l][:, :rw_lo], w_in[l][:, rw_hi:], w_in[l][:, rw_lo:rw_hi], pad], axis=1)
    mu = rwkv_mu[l]
    zeros_lora = jnp.zeros((DECAY_LORA, RWKV_WIDTH), F32)

    def blockdiag(up):
        return jnp.concatenate([jnp.concatenate([up[0], zeros_lora], axis=1),
                                jnp.concatenate([zeros_lora, up[1]], axis=1)], axis=0).astype(BF16)

    rp = {
        "mu_r": mu[None, 0:RWKV_WIDTH], "mu_k": mu[None, RWKV_WIDTH:2 * RWKV_WIDTH],
        "mu_v": mu[None, 2 * RWKV_WIDTH:3 * RWKV_WIDTH],
        "mu_z": jnp.pad(mu[3 * RWKV_WIDTH:], (0, Z_WIDTH - (RWKV_IN - 3 * RWKV_WIDTH)))[None],
        "k_k": k_k[l][None], "k_a": k_a[l][None], "r_k": r_k[l].reshape(1, RWKV_WIDTH),
        "w0": w0[l], "a0": a0[l], "w_up": blockdiag(w_up[l]), "a_up": blockdiag(a_up[l]),
        "g_up": jnp.pad(g_up[l], ((0, 2 * LANES - GATE_LORA), (0, 0))).astype(BF16),
        "ones_bd": _block_ones(),
    }
    wr = jnp.pad(w_router[l], ((0, 0), (0, LANES - N_EXPERTS)))
    br = jnp.pad(b_router[l], (0, LANES - N_EXPERTS))[None]
    return {"w_cat": w_cat.astype(BF16), "w_out": w_out[l].astype(BF16), "rp": rp,
            "sgu_w": sgu_w[l].astype(BF16), "sgu_bt": sgu_b[l].T, "router": (wr, br)}


def _layer(x, l, lay, mod48, lw, p, cache_k, cache_v, h0_lat, rope):
    n_ctx_rows = lay.n_ctx * lay.t_ctx
    h = _norm_mod(x, p["norm_mix"][l][None], mod48, 0, lay)
    proj = _matmul(h, lw["w_cat"], 1024 if x.shape[0] % 1024 == 0 else 512, 512)

    qn, kn = p["q_norm"][l][None], p["k_norm"][l][None]
    att_ctx, k_ctx = _attention_ctx(proj, qn, kn, lay)
    att_lat = _attention_lat(proj, cache_k, cache_v, rope[0], rope[1], qn, kn, l, lay)
    v_ctx = proj[:n_ctx_rows, COL_V:COL_V + KV_WIDTH]

    prep = _rwkv_prep(proj, lw["rp"], lay)
    lnw, lnb, ones_bd = p["lnx_w"][l][None], p["lnx_b"][l][None], lw["rp"]["ones_bd"]
    zeros_h = jnp.zeros((lay.n_ctx, RWKV_PAIRS, LANES, LANES), F32)
    rw_ctx, hf_ctx, hb_ctx = _rwkv_scan(prep, lnw, lnb, ones_bd, zeros_h, zeros_h, lay.n_ctx, lay.t_ctx, 0)
    rw_lat, _, _ = _rwkv_scan(prep, lnw, lnb, ones_bd, h0_lat[0], h0_lat[1], lay.n_lat, lay.t_lat, n_ctx_rows)

    sg = _sgu(proj, lw["sgu_w"], lw["sgu_bt"], p["sgu_norm"][l][None])

    att = jnp.concatenate([att_ctx, att_lat], axis=0)
    rw = jnp.concatenate([rw_ctx, rw_lat], axis=0)
    x = _out_proj(att, rw, sg, lw["w_out"], x, mod48, lay)

    h2, gates, sel = _norm_mod(x, p["norm_ffn"][l][None], mod48, 3, lay, router=lw["router"])
    src, gate_sorted, pos4, tiles = _route(sel[:, :N_EXPERTS], gates[:, :N_EXPERTS])
    xs = jnp.take(h2, src, axis=0)
    out_sorted = _moe_experts(xs, gate_sorted, tiles, p["w_exp1"], p["b_exp1r"], p["w_exp2"], p["b_exp2r"], l)
    y = jnp.sum(jnp.take(out_sorted, pos4, axis=0), axis=1)
    final = p["final_norm"][None] if l == DEPTH - 1 else None
    x = _gated_residual(x, y, mod48, lay, final)
    return x, k_ctx, v_ctx, hf_ctx, hb_ctx


def _forward(lay, x_prompt, x_sample, cache_k, cache_v, state_wkv_fwd, state_wkv_bwd, c, c_ctx, p):
    n_ctx_rows = lay.n_ctx * lay.t_ctx
    x = jnp.concatenate([x_prompt.reshape(n_ctx_rows, D_MODEL), x_sample.reshape(lay.n_lat * lay.t_lat, D_MODEL)])
    cond8 = jnp.zeros((8, D_MODEL), F32).at[0].set(c_ctx).at[1:1 + lay.n_lat].set(c)
    rope = _rope_tables(lay.t_lat)
    ck = cache_k.reshape(lay.n_lat, DEPTH, lay.past, KV_WIDTH)
    cv = cache_v.reshape(lay.n_lat, DEPTH, lay.past, KV_WIDTH)
    p = dict(p)
    p["b_exp1r"] = p["b_exp1"].reshape(DEPTH * N_EXPERTS, 1, 2 * EXPERT_FF)
    p["b_exp2r"] = p["b_exp2"].reshape(DEPTH * N_EXPERTS, 1, D_MODEL)
    ks, vs, sfs, sbs = [], [], [], []
    for l in range(DEPTH):
        lw = _layer_weights(l, p["w_in"], p["w_out"], p["rwkv_mu"], p["w0"], p["w_up"], p["a0"], p["a_up"],
                            p["g_up"], p["k_k"], p["k_a"], p["r_k"], p["sgu_w"], p["sgu_b"],
                            p["w_router"], p["b_router"])
        mod = _modulation(cond8, p["w_mod"], p["b_mod"], l)
        mod48 = mod.reshape(8 * 6, 1, D_MODEL)
        h0_lat = (_state_to_blockdiag(state_wkv_fwd[:, l]), _state_to_blockdiag(state_wkv_bwd[:, l]))
        x, k_l, v_l, hf, hb = _layer(x, l, lay, mod48, lw, p, ck, cv, h0_lat, rope)
        ks.append(k_l.reshape(lay.n_ctx, lay.t_ctx, ATT_KV_HEADS, HEAD_DIM))
        vs.append(v_l.reshape(lay.n_ctx, lay.t_ctx, ATT_KV_HEADS, HEAD_DIM))
        sfs.append(_blockdiag_to_state(hf))
        sbs.append(_blockdiag_to_state(hb))
    y_prompt = x[:n_ctx_rows].reshape(x_prompt.shape)
    y_sample = x[n_ctx_rows:].reshape(x_sample.shape)
    return (y_prompt, y_sample, jnp.stack(ks, axis=1), jnp.stack(vs, axis=1),
            jnp.stack(sfs, axis=1), jnp.stack(sbs, axis=1))


def kernel(x_prompt, x_sample, cache_k, cache_v, state_wkv_fwd, state_wkv_bwd, c, c_ctx, norm_mix, norm_ffn, w_mod, b_mod, w_in, w_out, q_norm, k_norm, rwkv_mu, w0, w_up, a0, a_up, g_up, k_k, k_a, r_k, lnx_w, lnx_b, sgu_norm, sgu_w, sgu_b, w_router, b_router, w_exp1, b_exp1, w_exp2, b_exp2, final_norm):
    lay = Layout(n_ctx=x_prompt.shape[0], t_ctx=x_prompt.shape[1], n_lat=x_sample.shape[0],
                 t_lat=x_sample.shape[1], past=cache_k.shape[2])
    p = dict(norm_mix=norm_mix, norm_ffn=norm_ffn, w_mod=w_mod, b_mod=b_mod, w_in=w_in, w_out=w_out,
             q_norm=q_norm, k_norm=k_norm, rwkv_mu=rwkv_mu, w0=w0, w_up=w_up, a0=a0, a_up=a_up, g_up=g_up,
             k_k=k_k, k_a=k_a, r_k=r_k, lnx_w=lnx_w, lnx_b=lnx_b, sgu_norm=sgu_norm, sgu_w=sgu_w, sgu_b=sgu_b,
             w_router=w_router, b_router=b_router, w_exp1=w_exp1, b_exp1=b_exp1, w_exp2=w_exp2, b_exp2=b_exp2,
             final_norm=final_norm)
    return _forward(lay, x_prompt, x_sample, cache_k, cache_v, state_wkv_fwd, state_wkv_bwd, c, c_ctx, p)
```

```python
import collections
import functools

import numpy as np
import jax
import jax.numpy as jnp
from jax import lax
from jax.experimental import pallas as pl
from jax.experimental.pallas import tpu as pltpu

F32 = jnp.float32
BF16 = jnp.bfloat16
HIGHEST = lax.Precision.HIGHEST

D_MODEL = 4096
DEPTH = 2
GRID_W = 64
HEAD_DIM = 128
ATT_WIDTH = 2048
ATT_KV_HEADS = 4
ATT_GROUP = 4
KV_WIDTH = 512
ROPE_THETA = 10000.0
ROPE_FREQS = 32
RWKV_WIDTH = 1024
RWKV_HEAD = 64
RWKV_PAIRS = 8
DECAY_LORA = 64
ICLR_LORA = 64
GATE_LORA = 160
RWKV_IN = 3 * RWKV_WIDTH + 2 * DECAY_LORA + 2 * ICLR_LORA + GATE_LORA
SGU_WIDTH = 1024
SGU_GROUPS = 8
CHUNK = 128
N_EXPERTS = 32
TOP_K = 4
EXPERT_FF = 2048
SWIGLU_LIMIT = 7.0
SWIGLU_ALPHA = 1.702
NORM_EPS = 1e-6
LNX_EPS = 64e-5
L2_EPS = 1e-12

LANES = 128
ROW_TILE = 256
SCAN_CHUNK = 64
SCAN_PAIRS_CTX = 4
SCAN_PAIRS_LAT = 2
VMEM_LIMIT = 48 * 1024 * 1024

COL_Q, COL_K, COL_V, COL_U, COL_SV, COL_R, COL_RK, COL_RV, COL_Z = 0, 2048, 2560, 3072, 4096, 5120, 6144, 7168, 8192
Z_WIDTH = 512
P_WIDTH = COL_Z + Z_WIDTH

MOE_TM = 256
MOE_TN1 = 512
MOE_TN2 = 1024
MOE_VMEM_LIMIT = 56 * 1024 * 1024

Layout = collections.namedtuple("Layout", "n_ctx t_ctx n_lat t_lat past")


def _params(n_axes, vmem_limit=VMEM_LIMIT):
    return pltpu.CompilerParams(dimension_semantics=("arbitrary",) * n_axes, vmem_limit_bytes=vmem_limit)


def _rows(lay):
    return lay.n_ctx * lay.t_ctx + lay.n_lat * lay.t_lat


def _tile_mod_rows(lay, tile):
    n_ctx_rows = lay.n_ctx * lay.t_ctx
    starts = np.arange(0, _rows(lay), tile)
    return np.where(starts < n_ctx_rows, 0, 1 + (starts - n_ctx_rows) // lay.t_lat).astype(np.int32)


def _mod_kernel(c_ref, w_ref, b_ref, o_ref):
    c = c_ref[...]
    a = (c * jax.nn.sigmoid(c)).astype(BF16)
    o_ref[...] = jnp.dot(a, w_ref[...].astype(BF16), preferred_element_type=F32) + b_ref[...]


def _modulation(cond8, w_mod, b_mod, layer):
    tn = 512
    n = w_mod.shape[-1]
    return pl.pallas_call(
        _mod_kernel,
        out_shape=jax.ShapeDtypeStruct((8, n), F32),
        grid=(n // tn,),
        in_specs=[
            pl.BlockSpec((8, D_MODEL), lambda j: (0, 0)),
            pl.BlockSpec((None, D_MODEL, tn), lambda j: (layer, 0, j)),
            pl.BlockSpec((None, 1, tn), lambda j: (layer, 0, j)),
        ],
        out_specs=pl.BlockSpec((8, tn), lambda j: (0, j)),
        compiler_params=_params(1),
        name="modulation",
    )(cond8, w_mod, b_mod.reshape(DEPTH, 1, n))


def _modulated_norm(x_ref, g_ref, sh_ref, sc_ref):
    x = x_ref[...]
    y = x * lax.rsqrt(jnp.mean(x * x, axis=-1, keepdims=True) + NORM_EPS) * g_ref[...]
    return y * (1.0 + sc_ref[0]) + sh_ref[0]


def _norm_mod_kernel(tm_ref, x_ref, g_ref, sh_ref, sc_ref, o_ref):
    o_ref[...] = _modulated_norm(x_ref, g_ref, sh_ref, sc_ref).astype(o_ref.dtype)


def _norm_router_kernel(tm_ref, x_ref, g_ref, sh_ref, sc_ref, wr_ref, br_ref, o_ref, gate_ref, idx_ref, sel_ref):
    h = _modulated_norm(x_ref, g_ref, sh_ref, sc_ref)
    o_ref[...] = h.astype(o_ref.dtype)
    logits = jnp.dot(h, wr_ref[...], precision=HIGHEST, preferred_element_type=F32) + br_ref[...]
    lane = lax.broadcasted_iota(jnp.int32, logits.shape, 1)
    neg = jnp.float32(-jnp.inf)
    work = jnp.where(lane < N_EXPERTS, logits, neg)
    tops, firsts = [], []
    sel = jnp.zeros_like(logits)
    for _ in range(TOP_K):
        m = jnp.max(work, axis=-1, keepdims=True)
        first = jnp.min(jnp.where(work == m, lane, LANES), axis=-1, keepdims=True)
        hot = lane == first
        tops.append(m)
        firsts.append(first)
        sel = jnp.where(hot, 1.0, sel)
        work = jnp.where(hot, neg, work)
    exps = [jnp.exp(t - tops[0]) for t in tops]
    denom = exps[0] + exps[1] + exps[2] + exps[3]
    gates = jnp.zeros_like(logits)
    idx = jnp.zeros(logits.shape, jnp.int32)
    for k in range(TOP_K):
        gates = jnp.where(lane == k, exps[k] / denom, gates)
        idx = jnp.where(lane == k, firsts[k], idx)
    gate_ref[...] = gates
    idx_ref[...] = idx
    sel_ref[...] = sel


def _norm_mod(x, gain, mod48, which_shift, lay, router=None):
    n = x.shape[0]
    tile_mod = jnp.asarray(_tile_mod_rows(lay, ROW_TILE))
    in_specs = [
        pl.BlockSpec((ROW_TILE, D_MODEL), lambda i, tm: (i, 0)),
        pl.BlockSpec((1, D_MODEL), lambda i, tm: (0, 0)),
        pl.BlockSpec((1, 1, D_MODEL), lambda i, tm: (tm[i] * 6 + which_shift, 0, 0)),
        pl.BlockSpec((1, 1, D_MODEL), lambda i, tm: (tm[i] * 6 + which_shift + 1, 0, 0)),
    ]
    h_spec = pl.BlockSpec((ROW_TILE, D_MODEL), lambda i, tm: (i, 0))
    if router is None:
        return pl.pallas_call(
            _norm_mod_kernel,
            out_shape=jax.ShapeDtypeStruct((n, D_MODEL), BF16),
            grid_spec=pltpu.PrefetchScalarGridSpec(
                num_scalar_prefetch=1, grid=(n // ROW_TILE,), in_specs=in_specs, out_specs=h_spec),
            compiler_params=_params(1),
            name="norm_mod",
        )(tile_mod, x, gain, mod48, mod48)
    wr, br = router
    lane_spec = pl.BlockSpec((ROW_TILE, LANES), lambda i, tm: (i, 0))
    return pl.pallas_call(
        _norm_router_kernel,
        out_shape=(jax.ShapeDtypeStruct((n, D_MODEL), BF16),
                   jax.ShapeDtypeStruct((n, LANES), F32),
                   jax.ShapeDtypeStruct((n, LANES), jnp.int32),
                   jax.ShapeDtypeStruct((n, LANES), F32)),
        grid_spec=pltpu.PrefetchScalarGridSpec(
            num_scalar_prefetch=1, grid=(n // ROW_TILE,),
            in_specs=in_specs + [pl.BlockSpec((D_MODEL, LANES), lambda i, tm: (0, 0)),
                                 pl.BlockSpec((1, LANES), lambda i, tm: (0, 0))],
            out_specs=(h_spec, lane_spec, lane_spec, lane_spec)),
        compiler_params=_params(1),
        name="norm_router",
    )(tile_mod, x, gain, mod48, mod48, wr, br)


def _mm_kernel(a_ref, w_ref, o_ref):
    o_ref[...] = jnp.dot(a_ref[...], w_ref[...], preferred_element_type=F32)


def _matmul(a, w, tm, tn):
    m, k = a.shape
    n = w.shape[1]
    return pl.pallas_call(
        _mm_kernel,
        out_shape=jax.ShapeDtypeStruct((m, n), F32),
        grid=(m // tm, n // tn),
        in_specs=[pl.BlockSpec((tm, k), lambda i, j: (i, 0)),
                  pl.BlockSpec((k, tn), lambda i, j: (0, j))],
        out_specs=pl.BlockSpec((tm, tn), lambda i, j: (i, j)),
        compiler_params=_params(2),
        name="in_proj",
    )(a, w)


def _head_norm(x, g):
    return x * lax.rsqrt(jnp.mean(x * x, axis=-1, keepdims=True) + NORM_EPS) * g


def _rope(x, cos, sin_signed):
    lane = lax.broadcasted_iota(jnp.int32, x.shape, 1)
    partner = jnp.where((lane & 32) == 0, pltpu.roll(x, LANES - 32, 1), pltpu.roll(x, 32, 1))
    return x * cos + partner * sin_signed


def _softmax_pv(q_bf, k_bf, v_bf):
    s = lax.dot_general(q_bf, k_bf, (((1,), (1,)), ((), ())), preferred_element_type=F32) * (HEAD_DIM ** -0.5)
    e = jnp.exp(s - jnp.max(s, axis=-1, keepdims=True))
    o = jnp.dot(e.astype(BF16), v_bf, preferred_element_type=F32)
    return o / jnp.sum(e, axis=-1, keepdims=True)


def _attn_ctx_kernel(q_ref, k_ref, v_ref, qn_ref, kn_ref, o_ref, ko_ref):
    t = q_ref.shape[0]
    kn = _head_norm(k_ref[...], kn_ref[...])
    ko_ref[...] = kn
    qs = [_head_norm(q_ref[:, h * HEAD_DIM:(h + 1) * HEAD_DIM], qn_ref[...]) for h in range(ATT_GROUP)]
    q = jnp.concatenate(qs, axis=0).astype(BF16)
    o = _softmax_pv(q, kn.astype(BF16), v_ref[...].astype(BF16))
    for h in range(ATT_GROUP):
        o_ref[:, h * HEAD_DIM:(h + 1) * HEAD_DIM] = o[h * t:(h + 1) * t].astype(o_ref.dtype)


def _attention_ctx(proj, qn, kn, lay):
    t = lay.t_ctx
    rows = lay.n_ctx * t
    kcol, vcol = COL_K // HEAD_DIM, COL_V // HEAD_DIM
    return pl.pallas_call(
        _attn_ctx_kernel,
        out_shape=(jax.ShapeDtypeStruct((rows, ATT_WIDTH), BF16), jax.ShapeDtypeStruct((rows, KV_WIDTH), F32)),
        grid=(lay.n_ctx, ATT_KV_HEADS),
        in_specs=[pl.BlockSpec((t, ATT_GROUP * HEAD_DIM), lambda b, g: (b, g)),
                  pl.BlockSpec((t, HEAD_DIM), lambda b, g: (b, kcol + g)),
                  pl.BlockSpec((t, HEAD_DIM), lambda b, g: (b, vcol + g)),
                  pl.BlockSpec((1, HEAD_DIM), lambda b, g: (0, 0)),
                  pl.BlockSpec((1, HEAD_DIM), lambda b, g: (0, 0))],
        out_specs=(pl.BlockSpec((t, ATT_GROUP * HEAD_DIM), lambda b, g: (b, g)),
                   pl.BlockSpec((t, HEAD_DIM), lambda b, g: (b, g))),
        compiler_params=_params(2),
        name="attention_ctx",
    )(proj, proj, proj, qn, kn)


def _attn_lat_kernel(q_ref, k_ref, v_ref, ck_ref, cv_ref, cosq_ref, sinq_ref, cosk_ref, sink_ref,
                     qn_ref, kn_ref, o_ref, kbuf, vbuf):
    past = ck_ref.shape[0]
    tq = q_ref.shape[0]

    @pl.when(pl.program_id(2) == 0)
    def _():
        kr = _rope(_head_norm(k_ref[...], kn_ref[...]), cosk_ref[...], sink_ref[...])
        kbuf[0:past, :] = ck_ref[...].astype(BF16)
        kbuf[past:, :] = kr.astype(BF16)
        vbuf[0:past, :] = cv_ref[...].astype(BF16)
        vbuf[past:, :] = v_ref[...].astype(BF16)

    cos, sin = cosq_ref[...], sinq_ref[...]
    qs = [_rope(_head_norm(q_ref[:, h * HEAD_DIM:(h + 1) * HEAD_DIM], qn_ref[...]), cos, sin)
          for h in range(ATT_GROUP)]
    q = jnp.concatenate(qs, axis=0).astype(BF16)
    o = _softmax_pv(q, kbuf[...], vbuf[...])
    for h in range(ATT_GROUP):
        o_ref[:, h * HEAD_DIM:(h + 1) * HEAD_DIM] = o[h * tq:(h + 1) * tq].astype(o_ref.dtype)


def _attention_lat(proj, cache_k, cache_v, cos, sin, qn, kn, layer, lay):
    t, past = lay.t_lat, lay.past
    tq = ROW_TILE
    nq = t // tq
    base_q = lay.n_ctx * lay.t_ctx // tq
    base_t = lay.n_ctx * lay.t_ctx // t
    kcol, vcol = COL_K // HEAD_DIM, COL_V // HEAD_DIM
    cache_spec = pl.BlockSpec((None, None, past, HEAD_DIM), lambda b, g, i: (b, layer, 0, g))
    return pl.pallas_call(
        _attn_lat_kernel,
        out_shape=jax.ShapeDtypeStruct((lay.n_lat * t, ATT_WIDTH), BF16),
        grid=(lay.n_lat, ATT_KV_HEADS, nq),
        in_specs=[pl.BlockSpec((tq, ATT_GROUP * HEAD_DIM), lambda b, g, i: (base_q + b * nq + i, g)),
                  pl.BlockSpec((t, HEAD_DIM), lambda b, g, i: (base_t + b, kcol + g)),
                  pl.BlockSpec((t, HEAD_DIM), lambda b, g, i: (base_t + b, vcol + g)),
                  cache_spec, cache_spec,
                  pl.BlockSpec((tq, HEAD_DIM), lambda b, g, i: (i, 0)),
                  pl.BlockSpec((tq, HEAD_DIM), lambda b, g, i: (i, 0)),
                  pl.BlockSpec((t, HEAD_DIM), lambda b, g, i: (0, 0)),
                  pl.BlockSpec((t, HEAD_DIM), lambda b, g, i: (0, 0)),
                  pl.BlockSpec((1, HEAD_DIM), lambda b, g, i: (0, 0)),
                  pl.BlockSpec((1, HEAD_DIM), lambda b, g, i: (0, 0))],
        out_specs=pl.BlockSpec((tq, ATT_GROUP * HEAD_DIM), lambda b, g, i: (b * nq + i, g)),
        scratch_shapes=[pltpu.VMEM((past + t, HEAD_DIM), BF16), pltpu.VMEM((past + t, HEAD_DIM), BF16)],
        compiler_params=_params(3),
        name="attention_lat",
    )(proj, proj, proj, cache_k, cache_v, cos, sin, cos, sin, qn, kn)


def _rope_tables(t):
    pos = np.arange(t)
    inv = ROPE_THETA ** (-np.arange(ROPE_FREQS, dtype=np.float32) / ROPE_FREQS)
    ang_r = (pos // GRID_W).astype(np.float32)[:, None] * inv
    ang_c = (pos % GRID_W).astype(np.float32)[:, None] * inv
    cr, sr, cc, sc = np.cos(ang_r), np.sin(ang_r), np.cos(ang_c), np.sin(ang_c)
    cos = np.concatenate([cr, cr, cc, cc], axis=1)
    sin = np.concatenate([-sr, sr, -sc, sc], axis=1)
    return jnp.asarray(cos, F32), jnp.asarray(sin, F32)


def _sgu_kernel(u_ref, sv_ref, w_ref, bt_ref, g_ref, o_ref):
    for g in range(SGU_GROUPS):
        cols = slice(g * LANES, (g + 1) * LANES)
        sv = sv_ref[:, cols]
        svn = sv * lax.rsqrt(jnp.mean(sv * sv, axis=-1, keepdims=True) + NORM_EPS) * g_ref[:, cols]
        z = jnp.dot(w_ref[g], svn.astype(BF16), preferred_element_type=F32) + bt_ref[:, g:g + 1]
        o_ref[:, cols] = (u_ref[:, cols] * z).astype(o_ref.dtype)


def _sgu(proj, w_bf, b_t, gain):
    n = proj.shape[0]
    ucol, svcol = COL_U // SGU_WIDTH, COL_SV // SGU_WIDTH
    return pl.pallas_call(
        _sgu_kernel,
        out_shape=jax.ShapeDtypeStruct((n, SGU_WIDTH), BF16),
        grid=(n // CHUNK,),
        in_specs=[pl.BlockSpec((CHUNK, SGU_WIDTH), lambda i: (i, ucol)),
                  pl.BlockSpec((CHUNK, SGU_WIDTH), lambda i: (i, svcol)),
                  pl.BlockSpec((SGU_GROUPS, CHUNK, CHUNK), lambda i: (0, 0, 0)),
                  pl.BlockSpec((CHUNK, SGU_GROUPS), lambda i: (0, 0)),
                  pl.BlockSpec((1, SGU_WIDTH), lambda i: (0, 0))],
        out_specs=pl.BlockSpec((CHUNK, SGU_WIDTH), lambda i: (i, 0)),
        compiler_params=_params(1),
        name="sgu",
    )(proj, proj, w_bf, b_t, gain)


def _pair_sum(x, ones_bd):
    blocks = [jnp.dot(x[:, p * LANES:(p + 1) * LANES], ones_bd, precision=HIGHEST, preferred_element_type=F32)
              for p in range(x.shape[1] // LANES)]
    return jnp.concatenate(blocks, axis=1)


def _rwkv_prep_kernel(first_ref, last_ref,
                      r_ref, k_ref, v_ref, z_ref, rp_ref, kp_ref, vp_ref, zp_ref, rn_ref, kn_ref, vn_ref, zn_ref,
                      mur_ref, muk_ref, muv_ref, muz_ref, kk_ref, ka_ref, rk_ref, w0_ref, a0_ref,
                      wup_ref, aup_ref, gup_ref, ones_ref,
                      r_o, v_o, kk_o, lwf_o, lwb_o, bf_o, bb_o, kdf_o, kdb_o, bonus_o, gate_o):
    i = pl.program_id(0)
    has_prev = first_ref[i] == 0
    has_next = last_ref[i] == 0
    row = lax.broadcasted_iota(jnp.int32, (ROW_TILE, 1), 0)

    def mix(x_ref, p_ref, n_ref, mu_ref):
        x = x_ref[...]
        prev = jnp.where(row == 0, jnp.where(has_prev, p_ref[7:8, :], 0.0), pltpu.roll(x, 1, 0))
        nxt = jnp.where(row == ROW_TILE - 1, jnp.where(has_next, n_ref[0:1, :], 0.0),
                        pltpu.roll(x, ROW_TILE - 1, 0))
        return x + (0.5 * (prev + nxt) - x) * mu_ref[...]

    r = mix(r_ref, rp_ref, rn_ref, mur_ref)
    k = mix(k_ref, kp_ref, kn_ref, muk_ref)
    v = mix(v_ref, vp_ref, vn_ref, muv_ref)
    z = mix(z_ref, zp_ref, zn_ref, muz_ref)
    ones_bd = ones_ref[...]

    kk = k * kk_ref[...]
    kk = kk / jnp.maximum(jnp.sqrt(_pair_sum(kk * kk, ones_bd)), L2_EPS)

    lora_w = jnp.dot(jnp.tanh(z[:, 0:LANES]).astype(BF16), wup_ref[...], preferred_element_type=F32)
    lora_a = jnp.dot(z[:, LANES:2 * LANES].astype(BF16), aup_ref[...], preferred_element_type=F32)
    gate_o[...] = jnp.dot(jax.nn.sigmoid(z[:, 2 * LANES:4 * LANES]).astype(BF16), gup_ref[...],
                          preferred_element_type=F32)

    iclr_sum = jnp.zeros_like(k)
    for d, (lw_o, b_o, kd_o) in enumerate(((lwf_o, bf_o, kdf_o), (lwb_o, bb_o, kdb_o))):
        cols = slice(d * RWKV_WIDTH, (d + 1) * RWKV_WIDTH)
        x = -(w0_ref[d:d + 1, :] + lora_w[:, cols])
        softplus = jnp.maximum(x, 0.0) + jnp.log(1.0 + jnp.exp(-jnp.abs(x)))
        log_decay = -jnp.exp(-softplus - 0.5)
        iclr = jax.nn.sigmoid(a0_ref[d:d + 1, :] + lora_a[:, cols])
        kd = k * (1.0 + (iclr - 1.0) * ka_ref[...])
        b = kk * iclr
        iclr_sum = iclr_sum + iclr
        for p in range(RWKV_PAIRS):
            pc = slice(p * LANES, (p + 1) * LANES)
            lw_o[p] = log_decay[:, pc]
            b_o[p] = b[:, pc]
            kd_o[p] = kd[:, pc]
    for p in range(RWKV_PAIRS):
        pc = slice(p * LANES, (p + 1) * LANES)
        r_o[p] = r[:, pc]
        v_o[p] = v[:, pc]
        kk_o[p] = kk[:, pc]
    bonus_o[...] = _pair_sum(r * k * rk_ref[...] * (2.0 + (iclr_sum - 2.0) * ka_ref[...]), ones_bd) * v


def _seq_edge_flags(lay):
    starts = np.arange(0, _rows(lay), ROW_TILE)
    n_ctx_rows = lay.n_ctx * lay.t_ctx
    in_seq = np.where(starts < n_ctx_rows, starts % lay.t_ctx, (starts - n_ctx_rows) % lay.t_lat)
    seq_len = np.where(starts < n_ctx_rows, lay.t_ctx, lay.t_lat)
    first = (in_seq == 0).astype(np.int32)
    last = (in_seq + ROW_TILE == seq_len).astype(np.int32)
    return jnp.asarray(first), jnp.asarray(last)


def _rwkv_prep(proj, rp, lay):
    n = proj.shape[0]
    n_tiles = n // ROW_TILE
    first, last = _seq_edge_flags(lay)
    sub = ROW_TILE // 8
    wcols = {"r": (RWKV_WIDTH, COL_R // RWKV_WIDTH), "k": (RWKV_WIDTH, COL_RK // RWKV_WIDTH),
             "v": (RWKV_WIDTH, COL_RV // RWKV_WIDTH), "z": (Z_WIDTH, COL_Z // Z_WIDTH)}
    main = [pl.BlockSpec((ROW_TILE, w), functools.partial(lambda i, f, l, c: (i, c), c=c))
            for w, c in wcols.values()]
    prev = [pl.BlockSpec((8, w), functools.partial(lambda i, f, l, c: (jnp.maximum(i * sub - 1, 0), c), c=c))
            for w, c in wcols.values()]
    nxt = [pl.BlockSpec((8, w), functools.partial(
        lambda i, f, l, c: (jnp.minimum((i + 1) * sub, n_tiles * sub - 1), c), c=c)) for w, c in wcols.values()]

    def const(shape):
        return pl.BlockSpec(shape, lambda i, f, l: (0,) * len(shape))

    consts = [const((1, RWKV_WIDTH))] * 3 + [const((1, Z_WIDTH))] + [const((1, RWKV_WIDTH))] * 3 + \
        [const((2, RWKV_WIDTH))] * 2 + [const((LANES, 2 * RWKV_WIDTH))] * 2 + \
        [const((2 * LANES, RWKV_WIDTH)), const((LANES, LANES))]
    pair_out = jax.ShapeDtypeStruct((RWKV_PAIRS, n, LANES), F32)
    pair_spec = pl.BlockSpec((RWKV_PAIRS, ROW_TILE, LANES), lambda i, f, l: (0, i, 0))
    row_out = jax.ShapeDtypeStruct((n, RWKV_WIDTH), F32)
    row_spec = pl.BlockSpec((ROW_TILE, RWKV_WIDTH), lambda i, f, l: (i, 0))
    return pl.pallas_call(
        _rwkv_prep_kernel,
        out_shape=(pair_out,) * 9 + (row_out,) * 2,
        grid_spec=pltpu.PrefetchScalarGridSpec(
            num_scalar_prefetch=2, grid=(n_tiles,),
            in_specs=main + prev + nxt + consts,
            out_specs=(pair_spec,) * 9 + (row_spec,) * 2),
        compiler_params=_params(1),
        name="rwkv_prep",
    )(first, last, *([proj] * 12), rp["mu_r"], rp["mu_k"], rp["mu_v"], rp["mu_z"], rp["k_k"], rp["k_a"],
      rp["r_k"], rp["w0"], rp["a0"], rp["w_up"], rp["a_up"], rp["g_up"], rp["ones_bd"])


def _bf(x):
    return x.astype(BF16)


def _mm(a, b):
    return jnp.dot(_bf(a), _bf(b), preferred_element_type=F32)


def _mm_nt(a, b):
    return lax.dot_general(_bf(a), _bf(b), (((1,), (1,)), ((), ())), preferred_element_type=F32)


def _mm_tn(a, b):
    return jnp.dot(_bf(a.T), _bf(b), preferred_element_type=F32)


def _scan_blocks(units):
    c = SCAN_CHUNK
    two = 2 * c
    t_i = lax.broadcasted_iota(jnp.int32, (c, c), 0)
    s_i = lax.broadcasted_iota(jnp.int32, (c, c), 1)
    lane = lax.broadcasted_iota(jnp.int32, (c, LANES), 1)
    head0 = lane < RWKV_HEAD
    row2 = lax.broadcasted_iota(jnp.int32, (two, two), 0)
    col2 = lax.broadcasted_iota(jnp.int32, (two, two), 1)
    same_head = (row2 < c) == (col2 < c)
    eye = jnp.where(row2 == col2, 1.0, 0.0)
    rev = [u[7] for u in units]
    tri = {d: jnp.where((s_i >= t_i) if d else (s_i <= t_i), 1.0, 0.0).astype(BF16) for d in set(rev)}
    strict = {d: same_head & ((col2 > row2) if d else (col2 < row2)) for d in set(rev)}
    incl = {d: same_head & ((col2 >= row2) if d else (col2 <= row2)) for d in set(rev)}

    def stack(x):
        return jnp.concatenate([jnp.where(head0, x, 0.0), jnp.where(head0, 0.0, x)], axis=0)

    def each(fn, *cols):
        return [fn(*args) for args in zip(*cols)]

    s, r, v, kk, lw, b, kd = ([u[i] for u in units] for i in range(7))
    hi = each(_bf, lw)
    mid = each(lambda x, h: _bf(x - h.astype(F32)), lw, hi)
    lo = each(lambda x, h, m: _bf(x - h.astype(F32) - m.astype(F32)), lw, hi, mid)
    parts = each(lambda d, h, m, l: jnp.dot(tri[d], jnp.concatenate([h, m, l], axis=1),
                                            preferred_element_type=F32), rev, hi, mid, lo)
    cum = each(lambda x: x[:, 0:LANES] + x[:, LANES:2 * LANES] + x[:, 2 * LANES:3 * LANES], parts)
    total = each(lambda d, x: x[0:1, :] if d else x[c - 1:c, :], rev, cum)
    a_s = each(lambda x, cm, l: stack(-x * jnp.exp(cm - l)), kk, cum, lw)
    r_s = each(lambda x, cm: stack(x * jnp.exp(cm)), r, cum)
    g_inv = each(lambda cm: jnp.exp(-cm), cum)
    g_end = each(lambda tt, cm: jnp.exp(tt - cm), total, cum)
    b_s = each(lambda x, g: stack(x * g), b, g_inv)
    k_s = each(lambda x, g: stack(x * g), kd, g_inv)
    be_s = each(lambda x, g: stack(x * g), b, g_end)
    ke_s = each(lambda x, g: stack(x * g), kd, g_end)
    v_s = each(stack, v)

    a_ab = each(lambda d, x, y: jnp.where(strict[d], _mm_nt(x, y), 0.0), rev, a_s, b_s)
    a_ak = each(lambda d, x, y: jnp.where(strict[d], _mm_nt(x, y), 0.0), rev, a_s, k_s)
    a_rb = each(lambda d, x, y: jnp.where(incl[d], _mm_nt(x, y), 0.0), rev, r_s, b_s)
    a_rk = each(lambda d, x, y: jnp.where(incl[d], _mm_nt(x, y), 0.0), rev, r_s, k_s)

    inv = each(lambda x: eye + x, a_ab)
    power = a_ab
    w = each(_mm, a_ak, v_s)
    for _ in range(int(np.log2(c)) - 1):
        power = each(_mm, power, power)
        inv = each(lambda i, pw: i + _mm(pw, i), inv, power)

    p = each(_mm, inv, a_s)
    q = each(_mm, inv, w)
    qv = each(lambda x, y: jnp.concatenate([x, y], axis=0), q, v_s)
    g_mat = each(lambda x, y, z: x + _mm(y, z), r_s, a_rb, p)
    y0 = each(lambda x, y, z: _mm(jnp.concatenate([x, y], axis=1), z), a_rb, a_rk, qv)
    m_low = each(_mm_tn, be_s, p)
    n_t = each(lambda x, y, z: _mm_tn(x, jnp.concatenate([y, z], axis=0)), qv, be_s, ke_s)

    y = each(lambda g, st, z: _mm_nt(g, st) + z, g_mat, s, y0)
    s_new = each(lambda tt, st, m, n: jnp.exp(tt) * st + _mm_nt(st, m) + n, total, s, m_low, n_t)
    return s_new, each(lambda x: x[0:c, :] + x[c:two, :], y)


def _rwkv_scan_kernel(r_ref, v_ref, kk_ref, lwf_ref, lwb_ref, bf_ref, bb_ref, kdf_ref, kdb_ref,
                      bonus_ref, gate_ref, lnw_ref, lnb_ref, ones_ref, sf0_ref, sb0_ref,
                      o_ref, sf_o, sb_o, yf, yb, *, unroll):
    n_pairs, t = r_ref.shape[0], r_ref.shape[1]
    nc = t // SCAN_CHUNK
    sf_o[...] = sf0_ref[...]
    sb_o[...] = sb0_ref[...]

    def body(ci, carry):
        units, sinks = [], []
        for p in range(n_pairs):
            for reverse, lw_ref, b_ref, kd_ref, s_ref, y_ref in ((False, lwf_ref, bf_ref, kdf_ref, sf_o, yf),
                                                                (True, lwb_ref, bb_ref, kdb_ref, sb_o, yb)):
                cc = (nc - 1 - ci) if reverse else ci
                rows = pl.ds(pl.multiple_of(cc * SCAN_CHUNK, SCAN_CHUNK), SCAN_CHUNK)
                units.append((s_ref[p], r_ref[p, rows, :], v_ref[p, rows, :], kk_ref[p, rows, :],
                              lw_ref[p, rows, :], b_ref[p, rows, :], kd_ref[p, rows, :], reverse))
                sinks.append((s_ref, y_ref, p, rows))
        s_new, y = _scan_blocks(units)
        for (s_ref, y_ref, p, rows), s_val, y_val in zip(sinks, s_new, y):
            s_ref[p] = s_val
            y_ref[p, rows, :] = y_val
        return carry

    lax.fori_loop(0, nc, body, 0, unroll=unroll)

    ones_bd = ones_ref[...]
    for p in range(n_pairs):
        cols = slice(p * LANES, (p + 1) * LANES)
        y = yf[p] + yb[p]
        mean = jnp.dot(y, ones_bd, precision=HIGHEST, preferred_element_type=F32) * (1.0 / RWKV_HEAD)
        dev = y - mean
        var = jnp.dot(dev * dev, ones_bd, precision=HIGHEST, preferred_element_type=F32) * (1.0 / RWKV_HEAD)
        yn = dev * lax.rsqrt(var + LNX_EPS) * lnw_ref[:, cols] + lnb_ref[:, cols]
        o_ref[:, cols] = ((yn + bonus_ref[:, cols]) * gate_ref[:, cols]).astype(o_ref.dtype)


def _rwkv_scan(prep, lnx_w, lnx_b, ones_bd, s0f, s0b, n_seq, t, row_base, pairs_per_step, unroll):
    r, v, kk, lwf, lwb, bf, bb, kdf, kdb, bonus, gate = prep
    base = row_base // t
    pp = pairs_per_step
    width = pp * LANES
    pair_spec = pl.BlockSpec((pp, t, LANES), lambda s, p: (p, base + s, 0))
    row_spec = pl.BlockSpec((t, width), lambda s, p: (base + s, p))
    vec_spec = pl.BlockSpec((1, width), lambda s, p: (0, p))
    s_spec = pl.BlockSpec((None, pp, LANES, LANES), lambda s, p: (s, p, 0, 0))
    s_shape = jax.ShapeDtypeStruct((n_seq, RWKV_PAIRS, LANES, LANES), F32)
    return pl.pallas_call(
        functools.partial(_rwkv_scan_kernel, unroll=unroll),
        out_shape=(jax.ShapeDtypeStruct((n_seq * t, RWKV_WIDTH), BF16), s_shape, s_shape),
        grid=(n_seq, RWKV_PAIRS // pp),
        in_specs=[pair_spec] * 9 + [row_spec, row_spec, vec_spec, vec_spec,
                                    pl.BlockSpec((LANES, LANES), lambda s, p: (0, 0)), s_spec, s_spec],
        out_specs=(pl.BlockSpec((t, width), lambda s, p: (s, p)), s_spec, s_spec),
        scratch_shapes=[pltpu.VMEM((pp, t, LANES), F32), pltpu.VMEM((pp, t, LANES), F32)],
        compiler_params=_params(2),
        name="rwkv_scan",
    )(r, v, kk, lwf, lwb, bf, bb, kdf, kdb, bonus, gate, lnx_w, lnx_b, ones_bd, s0f, s0b)


def _state_to_blockdiag(s):
    n = s.shape[0]
    h = s.reshape(n, RWKV_PAIRS, 2, RWKV_HEAD, RWKV_HEAD)
    z = jnp.zeros_like(h[:, :, 0])
    top = jnp.concatenate([h[:, :, 0], z], axis=-1)
    bot = jnp.concatenate([z, h[:, :, 1]], axis=-1)
    return jnp.concatenate([top, bot], axis=-2)


def _blockdiag_to_state(h):
    n = h.shape[0]
    h0 = h[:, :, :RWKV_HEAD, :RWKV_HEAD]
    h1 = h[:, :, RWKV_HEAD:, RWKV_HEAD:]
    return jnp.stack([h0, h1], axis=2).reshape(n, 2 * RWKV_PAIRS, RWKV_HEAD, RWKV_HEAD)


def _out_proj_kernel(tm_ref, att_ref, rw_ref, sg_ref, w_ref, x_ref, g_ref, o_ref):
    acc = jnp.dot(att_ref[...], w_ref[0:ATT_WIDTH, :], preferred_element_type=F32)
    acc += jnp.dot(rw_ref[...], w_ref[ATT_WIDTH:ATT_WIDTH + RWKV_WIDTH, :], preferred_element_type=F32)
    acc += jnp.dot(sg_ref[...], w_ref[ATT_WIDTH + RWKV_WIDTH:, :], preferred_element_type=F32)
    o_ref[...] = x_ref[...] + g_ref[0] * acc


def _out_proj(att, rw, sg, w_bf, x, mod48, lay):
    n = x.shape[0]
    tm, tn = 512, 512
    tile_mod = jnp.asarray(_tile_mod_rows(lay, tm))
    return pl.pallas_call(
        _out_proj_kernel,
        out_shape=jax.ShapeDtypeStruct((n, D_MODEL), F32),
        grid_spec=pltpu.PrefetchScalarGridSpec(
            num_scalar_prefetch=1, grid=(n // tm, D_MODEL // tn),
            in_specs=[pl.BlockSpec((tm, ATT_WIDTH), lambda i, j, t: (i, 0)),
                      pl.BlockSpec((tm, RWKV_WIDTH), lambda i, j, t: (i, 0)),
                      pl.BlockSpec((tm, SGU_WIDTH), lambda i, j, t: (i, 0)),
                      pl.BlockSpec((D_MODEL, tn), lambda i, j, t: (0, j)),
                      pl.BlockSpec((tm, tn), lambda i, j, t: (i, j)),
                      pl.BlockSpec((1, 1, tn), lambda i, j, t: (t[i] * 6 + 2, 0, j))],
            out_specs=pl.BlockSpec((tm, tn), lambda i, j, t: (i, j))),
        compiler_params=_params(2),
        name="out_proj",
    )(tile_mod, att, rw, sg, w_bf, x, mod48)


def _moe_up_kernel(te_ref, tx_ref, tv_ref, tf_ref, x_ref, wg_ref, wl_ref, bg_ref, bl_ref, o_ref, wg_bf, wl_bf):
    i = pl.program_id(1)

    @pl.when(tf_ref[i] == 1)
    def _():
        wg_bf[...] = wg_ref[...].astype(BF16)
        wl_bf[...] = wl_ref[...].astype(BF16)

    @pl.when(tv_ref[i] == 1)
    def _():
        x = x_ref[...]
        z_glu = jnp.dot(x, wg_bf[...], preferred_element_type=F32) + bg_ref[...]
        z_lin = jnp.dot(x, wl_bf[...], preferred_element_type=F32) + bl_ref[...]
        z_glu = jnp.minimum(z_glu, SWIGLU_LIMIT)
        z_lin = jnp.clip(z_lin, -SWIGLU_LIMIT, SWIGLU_LIMIT)
        o_ref[...] = (z_glu * jax.nn.sigmoid(SWIGLU_ALPHA * z_glu) * (z_lin + 1.0)).astype(o_ref.dtype)


def _moe_down_kernel(te_ref, tx_ref, tv_ref, tf_ref, a_ref, w_ref, b_ref, o_ref, w_bf):
    i = pl.program_id(1)

    @pl.when(tf_ref[i] == 1)
    def _():
        w_bf[...] = w_ref[...].astype(BF16)

    @pl.when(tv_ref[i] == 1)
    def _():
        y = jnp.dot(a_ref[...], w_bf[...], preferred_element_type=F32) + b_ref[...]
        o_ref[...] = y.astype(o_ref.dtype)


def _moe_experts(xs, tiles, w1, b1, w2, b2, layer):
    te, tx, tv, tf = tiles
    r = xs.shape[0]
    n_tiles = r // MOE_TM
    nj1 = EXPERT_FF // MOE_TN1
    act = pl.pallas_call(
        _moe_up_kernel,
        out_shape=jax.ShapeDtypeStruct((r, EXPERT_FF), BF16),
        grid_spec=pltpu.PrefetchScalarGridSpec(
            num_scalar_prefetch=4, grid=(nj1, n_tiles),
            in_specs=[pl.BlockSpec((MOE_TM, D_MODEL), lambda j, i, te, tx, tv, tf: (tx[i], 0)),
                      pl.BlockSpec((None, None, D_MODEL, MOE_TN1), lambda j, i, te, tx, tv, tf: (layer, te[i], 0, j)),
                      pl.BlockSpec((None, None, D_MODEL, MOE_TN1),
                                   lambda j, i, te, tx, tv, tf: (layer, te[i], 0, nj1 + j)),
                      pl.BlockSpec((None, 1, MOE_TN1), lambda j, i, te, tx, tv, tf: (layer * N_EXPERTS + te[i], 0, j)),
                      pl.BlockSpec((None, 1, MOE_TN1),
                                   lambda j, i, te, tx, tv, tf: (layer * N_EXPERTS + te[i], 0, nj1 + j))],
            out_specs=pl.BlockSpec((MOE_TM, MOE_TN1), lambda j, i, te, tx, tv, tf: (tx[i], j)),
            scratch_shapes=[pltpu.VMEM((D_MODEL, MOE_TN1), BF16), pltpu.VMEM((D_MODEL, MOE_TN1), BF16)]),
        compiler_params=_params(2, MOE_VMEM_LIMIT),
        name="moe_up",
    )(te, tx, tv, tf, xs, w1, w1, b1, b1)
    nj2 = D_MODEL // MOE_TN2
    return pl.pallas_call(
        _moe_down_kernel,
        out_shape=jax.ShapeDtypeStruct((r, D_MODEL), BF16),
        grid_spec=pltpu.PrefetchScalarGridSpec(
            num_scalar_prefetch=4, grid=(nj2, n_tiles),
            in_specs=[pl.BlockSpec((MOE_TM, EXPERT_FF), lambda j, i, te, tx, tv, tf: (tx[i], 0)),
                      pl.BlockSpec((None, None, EXPERT_FF, MOE_TN2), lambda j, i, te, tx, tv, tf: (layer, te[i], 0, j)),
                      pl.BlockSpec((None, 1, MOE_TN2), lambda j, i, te, tx, tv, tf: (layer * N_EXPERTS + te[i], 0, j))],
            out_specs=pl.BlockSpec((MOE_TM, MOE_TN2), lambda j, i, te, tx, tv, tf: (tx[i], j)),
            scratch_shapes=[pltpu.VMEM((EXPERT_FF, MOE_TN2), BF16)]),
        compiler_params=_params(2, MOE_VMEM_LIMIT),
        name="moe_down",
    )(te, tx, tv, tf, act, w2, b2)


def _route(sel, idx4):
    n = sel.shape[0]
    r_max = n * TOP_K + N_EXPERTS * MOE_TM
    n_tiles = r_max // MOE_TM
    sel_i = sel.astype(jnp.int32)
    counts = jnp.sum(sel_i, axis=0)
    tiles_per = (counts + MOE_TM - 1) // MOE_TM
    tile_end = jnp.cumsum(tiles_per)
    offsets = (tile_end - tiles_per) * MOE_TM
    pos_all = offsets[None, :] + jnp.cumsum(sel_i, axis=0) - sel_i
    experts = jnp.arange(N_EXPERTS, dtype=jnp.int32)
    pos4 = jnp.sum(jnp.where(idx4[:, :, None] == experts[None, None, :], pos_all[:, None, :], 0), axis=-1)

    n_used = tile_end[-1]
    tile_ids = jnp.arange(n_tiles, dtype=jnp.int32)
    tx = jnp.minimum(tile_ids, n_used - 1)
    te = jnp.minimum(jnp.sum((tile_end[None, :] <= tx[:, None]).astype(jnp.int32), axis=1), N_EXPERTS - 1)
    tv = (tile_ids < n_used).astype(jnp.int32)
    tf = jnp.concatenate([jnp.ones((1,), jnp.int32), (te[1:] != te[:-1]).astype(jnp.int32)])

    toks = jnp.repeat(jnp.arange(n, dtype=jnp.int32), TOP_K)
    src = jnp.zeros((r_max,), jnp.int32).at[pos4.reshape(-1)].set(toks, unique_indices=True)
    return src, pos4, (te.astype(jnp.int32), tx, tv, tf)


def _expert_mix(x_ref, y_refs, gate_ref, g_ref):
    y = gate_ref[:, 0:1] * y_refs[0][...].astype(F32)
    for k in range(1, TOP_K):
        y += gate_ref[:, k:k + 1] * y_refs[k][...].astype(F32)
    return x_ref[...] + g_ref[0] * y


def _resid_kernel(tm_ref, x_ref, y0_ref, y1_ref, y2_ref, y3_ref, gate_ref, g_ref, o_ref):
    o_ref[...] = _expert_mix(x_ref, (y0_ref, y1_ref, y2_ref, y3_ref), gate_ref, g_ref)


def _resid_norm_kernel(tm_ref, x_ref, y0_ref, y1_ref, y2_ref, y3_ref, gate_ref, g_ref, fn_ref, o_ref):
    x = _expert_mix(x_ref, (y0_ref, y1_ref, y2_ref, y3_ref), gate_ref, g_ref)
    o_ref[...] = x * lax.rsqrt(jnp.mean(x * x, axis=-1, keepdims=True) + NORM_EPS) * fn_ref[...]


def _gated_residual(x, y4, gates, mod48, lay, final_norm=None):
    n = x.shape[0]
    nt = n // ROW_TILE
    tile_mod = jnp.asarray(_tile_mod_rows(lay, ROW_TILE))
    tile = pl.BlockSpec((ROW_TILE, D_MODEL), lambda i, t: (i, 0))
    y_specs = [pl.BlockSpec((ROW_TILE, D_MODEL), functools.partial(lambda i, t, k: (k * nt + i, 0), k=k))
               for k in range(TOP_K)]
    in_specs = [tile] + y_specs + [pl.BlockSpec((ROW_TILE, LANES), lambda i, t: (i, 0)),
                                   pl.BlockSpec((1, 1, D_MODEL), lambda i, t: (t[i] * 6 + 5, 0, 0))]
    args = [tile_mod, x, y4, y4, y4, y4, gates, mod48]
    kern = _resid_kernel
    if final_norm is not None:
        in_specs.append(pl.BlockSpec((1, D_MODEL), lambda i, t: (0, 0)))
        args.append(final_norm)
        kern = _resid_norm_kernel
    return pl.pallas_call(
        kern,
        out_shape=jax.ShapeDtypeStruct((n, D_MODEL), F32),
        grid_spec=pltpu.PrefetchScalarGridSpec(
            num_scalar_prefetch=1, grid=(n // ROW_TILE,), in_specs=in_specs, out_specs=tile),
        compiler_params=_params(1),
        name="gated_residual",
    )(*args)


def _block_ones():
    idx = np.arange(LANES) // RWKV_HEAD
    return jnp.asarray((idx[:, None] == idx[None, :]).astype(np.float32))


def _layer_weights(l, w_in, w_out, rwkv_mu, w0, w_up, a0, a_up, g_up, k_k, k_a, r_k, sgu_w, sgu_b, w_router, b_router):
    pad = jnp.zeros((D_MODEL, P_WIDTH - w_in.shape[-1]), w_in.dtype)
    rw_lo, rw_hi = ATT_WIDTH + 2 * KV_WIDTH, ATT_WIDTH + 2 * KV_WIDTH + RWKV_IN
    w_cat = jnp.concatenate([w_in[l][:, :rw_lo], w_in[l][:, rw_hi:], w_in[l][:, rw_lo:rw_hi], pad], axis=1)
    mu = rwkv_mu[l]
    zeros_lora = jnp.zeros((DECAY_LORA, RWKV_WIDTH), F32)

    def blockdiag(up):
        return jnp.concatenate([jnp.concatenate([up[0], zeros_lora], axis=1),
                                jnp.concatenate([zeros_lora, up[1]], axis=1)], axis=0).astype(BF16)

    rp = {
        "mu_r": mu[None, 0:RWKV_WIDTH], "mu_k": mu[None, RWKV_WIDTH:2 * RWKV_WIDTH],
        "mu_v": mu[None, 2 * RWKV_WIDTH:3 * RWKV_WIDTH],
        "mu_z": jnp.pad(mu[3 * RWKV_WIDTH:], (0, Z_WIDTH - (RWKV_IN - 3 * RWKV_WIDTH)))[None],
        "k_k": k_k[l][None], "k_a": k_a[l][None], "r_k": r_k[l].reshape(1, RWKV_WIDTH),
        "w0": w0[l], "a0": a0[l], "w_up": blockdiag(w_up[l]), "a_up": blockdiag(a_up[l]),
        "g_up": jnp.pad(g_up[l], ((0, 2 * LANES - GATE_LORA), (0, 0))).astype(BF16),
        "ones_bd": _block_ones(),
    }
    wr = jnp.pad(w_router[l], ((0, 0), (0, LANES - N_EXPERTS)))
    br = jnp.pad(b_router[l], (0, LANES - N_EXPERTS))[None]
    return {"w_cat": w_cat.astype(BF16), "w_out": w_out[l].astype(BF16), "rp": rp,
            "sgu_w": sgu_w[l].astype(BF16), "sgu_bt": sgu_b[l].T, "router": (wr, br)}


def _layer(x, l, lay, mod48, lw, p, cache_k, cache_v, h0_lat, rope):
    n_ctx_rows = lay.n_ctx * lay.t_ctx
    h = _norm_mod(x, p["norm_mix"][l][None], mod48, 0, lay)
    proj = _matmul(h, lw["w_cat"], 1024 if x.shape[0] % 1024 == 0 else 512, 512)

    qn, kn = p["q_norm"][l][None], p["k_norm"][l][None]
    att_ctx, k_ctx = _attention_ctx(proj, qn, kn, lay)
    att_lat = _attention_lat(proj, cache_k, cache_v, rope[0], rope[1], qn, kn, l, lay)
    v_ctx = proj[:n_ctx_rows, COL_V:COL_V + KV_WIDTH]

    prep = _rwkv_prep(proj, lw["rp"], lay)
    lnw, lnb, ones_bd = p["lnx_w"][l][None], p["lnx_b"][l][None], lw["rp"]["ones_bd"]
    zeros_h = jnp.zeros((lay.n_ctx, RWKV_PAIRS, LANES, LANES), F32)
    rw_ctx, hf_ctx, hb_ctx = _rwkv_scan(prep, lnw, lnb, ones_bd, zeros_h, zeros_h, lay.n_ctx, lay.t_ctx, 0,
                                        pairs_per_step=SCAN_PAIRS_CTX, unroll=1)
    rw_lat, _, _ = _rwkv_scan(prep, lnw, lnb, ones_bd, h0_lat[0], h0_lat[1], lay.n_lat, lay.t_lat, n_ctx_rows,
                              pairs_per_step=SCAN_PAIRS_LAT, unroll=1)

    sg = _sgu(proj, lw["sgu_w"], lw["sgu_bt"], p["sgu_norm"][l][None])

    att = jnp.concatenate([att_ctx, att_lat], axis=0)
    rw = jnp.concatenate([rw_ctx, rw_lat], axis=0)
    x = _out_proj(att, rw, sg, lw["w_out"], x, mod48, lay)

    h2, gates, idx, sel = _norm_mod(x, p["norm_ffn"][l][None], mod48, 3, lay, router=lw["router"])
    src, pos4, tiles = _route(sel[:, :N_EXPERTS], idx[:, :TOP_K])
    xs = jnp.take(h2, src, axis=0)
    out_sorted = _moe_experts(xs, tiles, p["w_exp1"], p["b_exp1r"], p["w_exp2"], p["b_exp2r"], l)
    y4 = jnp.take(out_sorted, pos4.T.reshape(-1), axis=0)
    final = p["final_norm"][None] if l == DEPTH - 1 else None
    x = _gated_residual(x, y4, gates, mod48, lay, final)
    return x, k_ctx, v_ctx, hf_ctx, hb_ctx


def _forward(lay, x_prompt, x_sample, cache_k, cache_v, state_wkv_fwd, state_wkv_bwd, c, c_ctx, p):
    n_ctx_rows = lay.n_ctx * lay.t_ctx
    x = jnp.concatenate([x_prompt.reshape(n_ctx_rows, D_MODEL), x_sample.reshape(lay.n_lat * lay.t_lat, D_MODEL)])
    cond8 = jnp.zeros((8, D_MODEL), F32).at[0].set(c_ctx).at[1:1 + lay.n_lat].set(c)
    rope = _rope_tables(lay.t_lat)
    ck = cache_k.reshape(lay.n_lat, DEPTH, lay.past, KV_WIDTH)
    cv = cache_v.reshape(lay.n_lat, DEPTH, lay.past, KV_WIDTH)
    p = dict(p)
    p["b_exp1r"] = p["b_exp1"].reshape(DEPTH * N_EXPERTS, 1, 2 * EXPERT_FF)
    p["b_exp2r"] = p["b_exp2"].reshape(DEPTH * N_EXPERTS, 1, D_MODEL)
    ks, vs, sfs, sbs = [], [], [], []
    for l in range(DEPTH):
        lw = _layer_weights(l, p["w_in"], p["w_out"], p["rwkv_mu"], p["w0"], p["w_up"], p["a0"], p["a_up"],
                            p["g_up"], p["k_k"], p["k_a"], p["r_k"], p["sgu_w"], p["sgu_b"],
                            p["w_router"], p["b_router"])
        mod = _modulation(cond8, p["w_mod"], p["b_mod"], l)
        mod48 = mod.reshape(8 * 6, 1, D_MODEL)
        h0_lat = (_state_to_blockdiag(state_wkv_fwd[:, l]), _state_to_blockdiag(state_wkv_bwd[:, l]))
        x, k_l, v_l, hf, hb = _layer(x, l, lay, mod48, lw, p, ck, cv, h0_lat, rope)
        ks.append(k_l.reshape(lay.n_ctx, lay.t_ctx, ATT_KV_HEADS, HEAD_DIM))
        vs.append(v_l.reshape(lay.n_ctx, lay.t_ctx, ATT_KV_HEADS, HEAD_DIM))
        sfs.append(_blockdiag_to_state(hf))
        sbs.append(_blockdiag_to_state(hb))
    y_prompt = x[:n_ctx_rows].reshape(x_prompt.shape)
    y_sample = x[n_ctx_rows:].reshape(x_sample.shape)
    return (y_prompt, y_sample, jnp.stack(ks, axis=1), jnp.stack(vs, axis=1),
            jnp.stack(sfs, axis=1), jnp.stack(sbs, axis=1))


def kernel(x_prompt, x_sample, cache_k, cache_v, state_wkv_fwd, state_wkv_bwd, c, c_ctx, norm_mix, norm_ffn, w_mod, b_mod, w_in, w_out, q_norm, k_norm, rwkv_mu, w0, w_up, a0, a_up, g_up, k_k, k_a, r_k, lnx_w, lnx_b, sgu_norm, sgu_w, sgu_b, w_router, b_router, w_exp1, b_exp1, w_exp2, b_exp2, final_norm):
    lay = Layout(n_ctx=x_prompt.shape[0], t_ctx=x_prompt.shape[1], n_lat=x_sample.shape[0],
                 t_lat=x_sample.shape[1], past=cache_k.shape[2])
    p = dict(norm_mix=norm_mix, norm_ffn=norm_ffn, w_mod=w_mod, b_mod=b_mod, w_in=w_in, w_out=w_out,
             q_norm=q_norm, k_norm=k_norm, rwkv_mu=rwkv_mu, w0=w0, w_up=w_up, a0=a0, a_up=a_up, g_up=g_up,
             k_k=k_k, k_a=k_a, r_k=r_k, lnx_w=lnx_w, lnx_b=lnx_b, sgu_norm=sgu_norm, sgu_w=sgu_w, sgu_b=sgu_b,
             w_router=w_router, b_router=b_router, w_exp1=w_exp1, b_exp1=b_exp1, w_exp2=w_exp2, b_exp2=b_exp2,
             final_norm=final_norm)
    return _forward(lay, x_prompt, x_sample, cache_k, cache_v, state_wkv_fwd, state_wkv_bwd, c, c_ctx, p)
```

```python
import collections
import functools

import numpy as np
import jax
import jax.numpy as jnp
from jax import lax
from jax.experimental import pallas as pl
from jax.experimental.pallas import tpu as pltpu

F32 = jnp.float32
BF16 = jnp.bfloat16
HIGHEST = lax.Precision.HIGHEST

D_MODEL = 4096
DEPTH = 2
GRID_W = 64
HEAD_DIM = 128
ATT_WIDTH = 2048
ATT_KV_HEADS = 4
ATT_GROUP = 4
KV_WIDTH = 512
ROPE_THETA = 10000.0
ROPE_FREQS = 32
RWKV_WIDTH = 1024
RWKV_HEAD = 64
RWKV_PAIRS = 8
DECAY_LORA = 64
ICLR_LORA = 64
GATE_LORA = 160
RWKV_IN = 3 * RWKV_WIDTH + 2 * DECAY_LORA + 2 * ICLR_LORA + GATE_LORA
SGU_WIDTH = 1024
SGU_GROUPS = 8
CHUNK = 128
N_EXPERTS = 32
TOP_K = 4
EXPERT_FF = 2048
SWIGLU_LIMIT = 7.0
SWIGLU_ALPHA = 1.702
NORM_EPS = 1e-6
LNX_EPS = 64e-5
L2_EPS = 1e-12

LANES = 128
ROW_TILE = 256
SCAN_CHUNK = 64
SCAN_PAIRS_CTX = 4
SCAN_PAIRS_LAT = 2
VMEM_LIMIT = 48 * 1024 * 1024

COL_Q, COL_K, COL_V, COL_R, COL_RK, COL_RV, COL_Z = 0, 2048, 2560, 3072, 4096, 5120, 6144
Z_WIDTH = 512
P_WIDTH = COL_Z + Z_WIDTH
SGU_IN_START = ATT_WIDTH + 2 * KV_WIDTH + RWKV_IN
COL_U, COL_SV = 0, 1024

MOE_TM = 256
MOE_TN1 = 512
MOE_TN2 = 1024
MOE_VMEM_LIMIT = 56 * 1024 * 1024

Layout = collections.namedtuple("Layout", "n_ctx t_ctx n_lat t_lat past")


def _params(n_axes, vmem_limit=VMEM_LIMIT):
    return pltpu.CompilerParams(dimension_semantics=("arbitrary",) * n_axes, vmem_limit_bytes=vmem_limit)


def _rows(lay):
    return lay.n_ctx * lay.t_ctx + lay.n_lat * lay.t_lat


def _tile_mod_rows(lay, tile):
    n_ctx_rows = lay.n_ctx * lay.t_ctx
    starts = np.arange(0, _rows(lay), tile)
    return np.where(starts < n_ctx_rows, 0, 1 + (starts - n_ctx_rows) // lay.t_lat).astype(np.int32)


def _mod_kernel(c_ref, w_ref, b_ref, o_ref):
    c = c_ref[...]
    a = (c * jax.nn.sigmoid(c)).astype(BF16)
    o_ref[...] = jnp.dot(a, w_ref[...].astype(BF16), preferred_element_type=F32) + b_ref[...]


def _modulation(cond8, w_mod, b_mod, layer):
    tn = 512
    n = w_mod.shape[-1]
    return pl.pallas_call(
        _mod_kernel,
        out_shape=jax.ShapeDtypeStruct((8, n), F32),
        grid=(n // tn,),
        in_specs=[
            pl.BlockSpec((8, D_MODEL), lambda j: (0, 0)),
            pl.BlockSpec((None, D_MODEL, tn), lambda j: (layer, 0, j)),
            pl.BlockSpec((None, 1, tn), lambda j: (layer, 0, j)),
        ],
        out_specs=pl.BlockSpec((8, tn), lambda j: (0, j)),
        compiler_params=_params(1),
        name="modulation",
    )(cond8, w_mod, b_mod.reshape(DEPTH, 1, n))


def _modulated_norm(x_ref, g_ref, sh_ref, sc_ref):
    x = x_ref[...]
    y = x * lax.rsqrt(jnp.mean(x * x, axis=-1, keepdims=True) + NORM_EPS) * g_ref[...]
    return y * (1.0 + sc_ref[0]) + sh_ref[0]


def _norm_mod_kernel(tm_ref, x_ref, g_ref, sh_ref, sc_ref, o_ref):
    o_ref[...] = _modulated_norm(x_ref, g_ref, sh_ref, sc_ref).astype(o_ref.dtype)


def _norm_router_kernel(tm_ref, x_ref, g_ref, sh_ref, sc_ref, wr_ref, br_ref, o_ref, gate_ref, idx_ref, sel_ref):
    h = _modulated_norm(x_ref, g_ref, sh_ref, sc_ref)
    o_ref[...] = h.astype(o_ref.dtype)
    logits = jnp.dot(h, wr_ref[...], precision=HIGHEST, preferred_element_type=F32) + br_ref[...]
    lane = lax.broadcasted_iota(jnp.int32, logits.shape, 1)
    neg = jnp.float32(-jnp.inf)
    work = jnp.where(lane < N_EXPERTS, logits, neg)
    tops, firsts = [], []
    sel = jnp.zeros_like(logits)
    for _ in range(TOP_K):
        m = jnp.max(work, axis=-1, keepdims=True)
        first = jnp.min(jnp.where(work == m, lane, LANES), axis=-1, keepdims=True)
        hot = lane == first
        tops.append(m)
        firsts.append(first)
        sel = jnp.where(hot, 1.0, sel)
        work = jnp.where(hot, neg, work)
    exps = [jnp.exp(t - tops[0]) for t in tops]
    denom = exps[0] + exps[1] + exps[2] + exps[3]
    gates = jnp.zeros_like(logits)
    idx = jnp.zeros(logits.shape, jnp.int32)
    for k in range(TOP_K):
        gates = jnp.where(lane == k, exps[k] / denom, gates)
        idx = jnp.where(lane == k, firsts[k], idx)
    gate_ref[...] = gates
    idx_ref[...] = idx
    sel_ref[...] = sel


def _norm_mod(x, gain, mod48, which_shift, lay, router=None):
    n = x.shape[0]
    tile_mod = jnp.asarray(_tile_mod_rows(lay, ROW_TILE))
    in_specs = [
        pl.BlockSpec((ROW_TILE, D_MODEL), lambda i, tm: (i, 0)),
        pl.BlockSpec((1, D_MODEL), lambda i, tm: (0, 0)),
        pl.BlockSpec((1, 1, D_MODEL), lambda i, tm: (tm[i] * 6 + which_shift, 0, 0)),
        pl.BlockSpec((1, 1, D_MODEL), lambda i, tm: (tm[i] * 6 + which_shift + 1, 0, 0)),
    ]
    h_spec = pl.BlockSpec((ROW_TILE, D_MODEL), lambda i, tm: (i, 0))
    if router is None:
        return pl.pallas_call(
            _norm_mod_kernel,
            out_shape=jax.ShapeDtypeStruct((n, D_MODEL), BF16),
            grid_spec=pltpu.PrefetchScalarGridSpec(
                num_scalar_prefetch=1, grid=(n // ROW_TILE,), in_specs=in_specs, out_specs=h_spec),
            compiler_params=_params(1),
            name="norm_mod",
        )(tile_mod, x, gain, mod48, mod48)
    wr, br = router
    lane_spec = pl.BlockSpec((ROW_TILE, LANES), lambda i, tm: (i, 0))
    return pl.pallas_call(
        _norm_router_kernel,
        out_shape=(jax.ShapeDtypeStruct((n, D_MODEL), BF16),
                   jax.ShapeDtypeStruct((n, LANES), F32),
                   jax.ShapeDtypeStruct((n, LANES), jnp.int32),
                   jax.ShapeDtypeStruct((n, LANES), F32)),
        grid_spec=pltpu.PrefetchScalarGridSpec(
            num_scalar_prefetch=1, grid=(n // ROW_TILE,),
            in_specs=in_specs + [pl.BlockSpec((D_MODEL, LANES), lambda i, tm: (0, 0)),
                                 pl.BlockSpec((1, LANES), lambda i, tm: (0, 0))],
            out_specs=(h_spec, lane_spec, lane_spec, lane_spec)),
        compiler_params=_params(1),
        name="norm_router",
    )(tile_mod, x, gain, mod48, mod48, wr, br)


def _mm_kernel(a_ref, w_ref, o_ref):
    o_ref[...] = jnp.dot(a_ref[...], w_ref[...].astype(BF16), preferred_element_type=F32)


def _matmul(a, w, layer, n, tm, tn):
    m, k = a.shape
    return pl.pallas_call(
        _mm_kernel,
        out_shape=jax.ShapeDtypeStruct((m, n), F32),
        grid=(m // tm, n // tn),
        in_specs=[pl.BlockSpec((tm, k), lambda i, j: (i, 0)),
                  pl.BlockSpec((None, k, tn), lambda i, j: (layer, 0, j))],
        out_specs=pl.BlockSpec((tm, tn), lambda i, j: (i, j)),
        compiler_params=_params(2),
        name="in_proj",
    )(a, w)


def _head_norm(x, g):
    return x * lax.rsqrt(jnp.mean(x * x, axis=-1, keepdims=True) + NORM_EPS) * g


def _rope(x, cos, sin_signed):
    lane = lax.broadcasted_iota(jnp.int32, x.shape, 1)
    partner = jnp.where((lane & 32) == 0, pltpu.roll(x, LANES - 32, 1), pltpu.roll(x, 32, 1))
    return x * cos + partner * sin_signed


def _softmax_pv(q_bf, k_bf, v_bf):
    s = lax.dot_general(q_bf, k_bf, (((1,), (1,)), ((), ())), preferred_element_type=F32) * (HEAD_DIM ** -0.5)
    e = jnp.exp(s - jnp.max(s, axis=-1, keepdims=True))
    o = jnp.dot(e.astype(BF16), v_bf, preferred_element_type=F32)
    return o / jnp.sum(e, axis=-1, keepdims=True)


def _attn_ctx_kernel(q_ref, k_ref, v_ref, qn_ref, kn_ref, o_ref, ko_ref):
    t = q_ref.shape[0]
    kn = _head_norm(k_ref[...], kn_ref[...])
    ko_ref[...] = kn
    qs = [_head_norm(q_ref[:, h * HEAD_DIM:(h + 1) * HEAD_DIM], qn_ref[...]) for h in range(ATT_GROUP)]
    q = jnp.concatenate(qs, axis=0).astype(BF16)
    o = _softmax_pv(q, kn.astype(BF16), v_ref[...].astype(BF16))
    for h in range(ATT_GROUP):
        o_ref[:, h * HEAD_DIM:(h + 1) * HEAD_DIM] = o[h * t:(h + 1) * t].astype(o_ref.dtype)


def _attention_ctx(proj, qn, kn, lay):
    t = lay.t_ctx
    rows = lay.n_ctx * t
    kcol, vcol = COL_K // HEAD_DIM, COL_V // HEAD_DIM
    return pl.pallas_call(
        _attn_ctx_kernel,
        out_shape=(jax.ShapeDtypeStruct((rows, ATT_WIDTH), BF16), jax.ShapeDtypeStruct((rows, KV_WIDTH), F32)),
        grid=(lay.n_ctx, ATT_KV_HEADS),
        in_specs=[pl.BlockSpec((t, ATT_GROUP * HEAD_DIM), lambda b, g: (b, g)),
                  pl.BlockSpec((t, HEAD_DIM), lambda b, g: (b, kcol + g)),
                  pl.BlockSpec((t, HEAD_DIM), lambda b, g: (b, vcol + g)),
                  pl.BlockSpec((1, HEAD_DIM), lambda b, g: (0, 0)),
                  pl.BlockSpec((1, HEAD_DIM), lambda b, g: (0, 0))],
        out_specs=(pl.BlockSpec((t, ATT_GROUP * HEAD_DIM), lambda b, g: (b, g)),
                   pl.BlockSpec((t, HEAD_DIM), lambda b, g: (b, g))),
        compiler_params=_params(2),
        name="attention_ctx",
    )(proj, proj, proj, qn, kn)


def _attn_lat_kernel(q_ref, k_ref, v_ref, ck_ref, cv_ref, cosq_ref, sinq_ref, cosk_ref, sink_ref,
                     qn_ref, kn_ref, o_ref, kbuf, vbuf):
    past = ck_ref.shape[0]
    tq = q_ref.shape[0]

    @pl.when(pl.program_id(2) == 0)
    def _():
        kr = _rope(_head_norm(k_ref[...], kn_ref[...]), cosk_ref[...], sink_ref[...])
        kbuf[0:past, :] = ck_ref[...].astype(BF16)
        kbuf[past:, :] = kr.astype(BF16)
        vbuf[0:past, :] = cv_ref[...].astype(BF16)
        vbuf[past:, :] = v_ref[...].astype(BF16)

    cos, sin = cosq_ref[...], sinq_ref[...]
    qs = [_rope(_head_norm(q_ref[:, h * HEAD_DIM:(h + 1) * HEAD_DIM], qn_ref[...]), cos, sin)
          for h in range(ATT_GROUP)]
    q = jnp.concatenate(qs, axis=0).astype(BF16)
    o = _softmax_pv(q, kbuf[...], vbuf[...])
    for h in range(ATT_GROUP):
        o_ref[:, h * HEAD_DIM:(h + 1) * HEAD_DIM] = o[h * tq:(h + 1) * tq].astype(o_ref.dtype)


def _attention_lat(proj, cache_k, cache_v, cos, sin, qn, kn, layer, lay):
    t, past = lay.t_lat, lay.past
    tq = ROW_TILE
    nq = t // tq
    base_q = lay.n_ctx * lay.t_ctx // tq
    base_t = lay.n_ctx * lay.t_ctx // t
    kcol, vcol = COL_K // HEAD_DIM, COL_V // HEAD_DIM
    cache_spec = pl.BlockSpec((None, None, past, HEAD_DIM), lambda b, g, i: (b, layer, 0, g))
    return pl.pallas_call(
        _attn_lat_kernel,
        out_shape=jax.ShapeDtypeStruct((lay.n_lat * t, ATT_WIDTH), BF16),
        grid=(lay.n_lat, ATT_KV_HEADS, nq),
        in_specs=[pl.BlockSpec((tq, ATT_GROUP * HEAD_DIM), lambda b, g, i: (base_q + b * nq + i, g)),
                  pl.BlockSpec((t, HEAD_DIM), lambda b, g, i: (base_t + b, kcol + g)),
                  pl.BlockSpec((t, HEAD_DIM), lambda b, g, i: (base_t + b, vcol + g)),
                  cache_spec, cache_spec,
                  pl.BlockSpec((tq, HEAD_DIM), lambda b, g, i: (i, 0)),
                  pl.BlockSpec((tq, HEAD_DIM), lambda b, g, i: (i, 0)),
                  pl.BlockSpec((t, HEAD_DIM), lambda b, g, i: (0, 0)),
                  pl.BlockSpec((t, HEAD_DIM), lambda b, g, i: (0, 0)),
                  pl.BlockSpec((1, HEAD_DIM), lambda b, g, i: (0, 0)),
                  pl.BlockSpec((1, HEAD_DIM), lambda b, g, i: (0, 0))],
        out_specs=pl.BlockSpec((tq, ATT_GROUP * HEAD_DIM), lambda b, g, i: (b * nq + i, g)),
        scratch_shapes=[pltpu.VMEM((past + t, HEAD_DIM), BF16), pltpu.VMEM((past + t, HEAD_DIM), BF16)],
        compiler_params=_params(3),
        name="attention_lat",
    )(proj, proj, proj, cache_k, cache_v, cos, sin, cos, sin, qn, kn)


def _rope_tables(t):
    pos = np.arange(t)
    inv = ROPE_THETA ** (-np.arange(ROPE_FREQS, dtype=np.float32) / ROPE_FREQS)
    ang_r = (pos // GRID_W).astype(np.float32)[:, None] * inv
    ang_c = (pos % GRID_W).astype(np.float32)[:, None] * inv
    cr, sr, cc, sc = np.cos(ang_r), np.sin(ang_r), np.cos(ang_c), np.sin(ang_c)
    cos = np.concatenate([cr, cr, cc, cc], axis=1)
    sin = np.concatenate([-sr, sr, -sc, sc], axis=1)
    return jnp.asarray(cos, F32), jnp.asarray(sin, F32)


def _sgu_kernel(u_ref, sv_ref, w_ref, bt_ref, g_ref, o_ref):
    for g in range(SGU_GROUPS):
        cols = slice(g * LANES, (g + 1) * LANES)
        sv = sv_ref[:, cols]
        svn = sv * lax.rsqrt(jnp.mean(sv * sv, axis=-1, keepdims=True) + NORM_EPS) * g_ref[:, cols]
        z = jnp.dot(w_ref[g], svn.astype(BF16), preferred_element_type=F32) + bt_ref[:, g:g + 1]
        o_ref[:, cols] = (u_ref[:, cols] * z).astype(o_ref.dtype)


def _sgu(proj, w_bf, b_t, gain):
    n = proj.shape[0]
    ucol, svcol = COL_U // SGU_WIDTH, COL_SV // SGU_WIDTH
    return pl.pallas_call(
        _sgu_kernel,
        out_shape=jax.ShapeDtypeStruct((n, SGU_WIDTH), BF16),
        grid=(n // CHUNK,),
        in_specs=[pl.BlockSpec((CHUNK, SGU_WIDTH), lambda i: (i, ucol)),
                  pl.BlockSpec((CHUNK, SGU_WIDTH), lambda i: (i, svcol)),
                  pl.BlockSpec((SGU_GROUPS, CHUNK, CHUNK), lambda i: (0, 0, 0)),
                  pl.BlockSpec((CHUNK, SGU_GROUPS), lambda i: (0, 0)),
                  pl.BlockSpec((1, SGU_WIDTH), lambda i: (0, 0))],
        out_specs=pl.BlockSpec((CHUNK, SGU_WIDTH), lambda i: (i, 0)),
        compiler_params=_params(1),
        name="sgu",
    )(proj, proj, w_bf, b_t, gain)


def _pair_sum(x, ones_bd):
    blocks = [jnp.dot(x[:, p * LANES:(p + 1) * LANES], ones_bd, precision=HIGHEST, preferred_element_type=F32)
              for p in range(x.shape[1] // LANES)]
    return jnp.concatenate(blocks, axis=1)


def _rwkv_prep_kernel(first_ref, last_ref,
                      r_ref, k_ref, v_ref, z_ref, rp_ref, kp_ref, vp_ref, zp_ref, rn_ref, kn_ref, vn_ref, zn_ref,
                      mur_ref, muk_ref, muv_ref, muz_ref, kk_ref, ka_ref, rk_ref, w0_ref, a0_ref,
                      wup_ref, aup_ref, gup_ref, ones_ref,
                      r_o, v_o, kk_o, lwf_o, lwb_o, bf_o, bb_o, kdf_o, kdb_o, bonus_o, gate_o):
    i = pl.program_id(0)
    has_prev = first_ref[i] == 0
    has_next = last_ref[i] == 0
    row = lax.broadcasted_iota(jnp.int32, (ROW_TILE, 1), 0)

    def mix(x_ref, p_ref, n_ref, mu_ref):
        x = x_ref[...]
        prev = jnp.where(row == 0, jnp.where(has_prev, p_ref[7:8, :], 0.0), pltpu.roll(x, 1, 0))
        nxt = jnp.where(row == ROW_TILE - 1, jnp.where(has_next, n_ref[0:1, :], 0.0),
                        pltpu.roll(x, ROW_TILE - 1, 0))
        return x + (0.5 * (prev + nxt) - x) * mu_ref[...]

    r = mix(r_ref, rp_ref, rn_ref, mur_ref)
    k = mix(k_ref, kp_ref, kn_ref, muk_ref)
    v = mix(v_ref, vp_ref, vn_ref, muv_ref)
    z = mix(z_ref, zp_ref, zn_ref, muz_ref)
    ones_bd = ones_ref[...]

    kk = k * kk_ref[...]
    kk = kk / jnp.maximum(jnp.sqrt(_pair_sum(kk * kk, ones_bd)), L2_EPS)

    lora_w = jnp.dot(jnp.tanh(z[:, 0:LANES]).astype(BF16), wup_ref[...], preferred_element_type=F32)
    lora_a = jnp.dot(z[:, LANES:2 * LANES].astype(BF16), aup_ref[...], preferred_element_type=F32)
    gate_o[...] = jnp.dot(jax.nn.sigmoid(z[:, 2 * LANES:4 * LANES]).astype(BF16), gup_ref[...],
                          preferred_element_type=F32)

    iclr_sum = jnp.zeros_like(k)
    for d, (lw_o, b_o, kd_o) in enumerate(((lwf_o, bf_o, kdf_o), (lwb_o, bb_o, kdb_o))):
        cols = slice(d * RWKV_WIDTH, (d + 1) * RWKV_WIDTH)
        x = -(w0_ref[d:d + 1, :] + lora_w[:, cols])
        softplus = jnp.maximum(x, 0.0) + jnp.log(1.0 + jnp.exp(-jnp.abs(x)))
        log_decay = -jnp.exp(-softplus - 0.5)
        iclr = jax.nn.sigmoid(a0_ref[d:d + 1, :] + lora_a[:, cols])
        kd = k * (1.0 + (iclr - 1.0) * ka_ref[...])
        b = kk * iclr
        iclr_sum = iclr_sum + iclr
        for p in range(RWKV_PAIRS):
            pc = slice(p * LANES, (p + 1) * LANES)
            lw_o[p] = log_decay[:, pc]
            b_o[p] = b[:, pc]
            kd_o[p] = kd[:, pc]
    for p in range(RWKV_PAIRS):
        pc = slice(p * LANES, (p + 1) * LANES)
        r_o[p] = r[:, pc]
        v_o[p] = v[:, pc]
        kk_o[p] = kk[:, pc]
    bonus_o[...] = _pair_sum(r * k * rk_ref[...] * (2.0 + (iclr_sum - 2.0) * ka_ref[...]), ones_bd) * v


def _seq_edge_flags(lay):
    starts = np.arange(0, _rows(lay), ROW_TILE)
    n_ctx_rows = lay.n_ctx * lay.t_ctx
    in_seq = np.where(starts < n_ctx_rows, starts % lay.t_ctx, (starts - n_ctx_rows) % lay.t_lat)
    seq_len = np.where(starts < n_ctx_rows, lay.t_ctx, lay.t_lat)
    first = (in_seq == 0).astype(np.int32)
    last = (in_seq + ROW_TILE == seq_len).astype(np.int32)
    return jnp.asarray(first), jnp.asarray(last)


def _rwkv_prep(proj, rp, lay):
    n = proj.shape[0]
    n_tiles = n // ROW_TILE
    first, last = _seq_edge_flags(lay)
    sub = ROW_TILE // 8
    wcols = {"r": (RWKV_WIDTH, COL_R // RWKV_WIDTH), "k": (RWKV_WIDTH, COL_RK // RWKV_WIDTH),
             "v": (RWKV_WIDTH, COL_RV // RWKV_WIDTH), "z": (Z_WIDTH, COL_Z // Z_WIDTH)}
    main = [pl.BlockSpec((ROW_TILE, w), functools.partial(lambda i, f, l, c: (i, c), c=c))
            for w, c in wcols.values()]
    prev = [pl.BlockSpec((8, w), functools.partial(lambda i, f, l, c: (jnp.maximum(i * sub - 1, 0), c), c=c))
            for w, c in wcols.values()]
    nxt = [pl.BlockSpec((8, w), functools.partial(
        lambda i, f, l, c: (jnp.minimum((i + 1) * sub, n_tiles * sub - 1), c), c=c)) for w, c in wcols.values()]

    def const(shape):
        return pl.BlockSpec(shape, lambda i, f, l: (0,) * len(shape))

    consts = [const((1, RWKV_WIDTH))] * 3 + [const((1, Z_WIDTH))] + [const((1, RWKV_WIDTH))] * 3 + \
        [const((2, RWKV_WIDTH))] * 2 + [const((LANES, 2 * RWKV_WIDTH))] * 2 + \
        [const((2 * LANES, RWKV_WIDTH)), const((LANES, LANES))]
    pair_out = jax.ShapeDtypeStruct((RWKV_PAIRS, n, LANES), F32)
    pair_spec = pl.BlockSpec((RWKV_PAIRS, ROW_TILE, LANES), lambda i, f, l: (0, i, 0))
    row_out = jax.ShapeDtypeStruct((n, RWKV_WIDTH), F32)
    row_spec = pl.BlockSpec((ROW_TILE, RWKV_WIDTH), lambda i, f, l: (i, 0))
    return pl.pallas_call(
        _rwkv_prep_kernel,
        out_shape=(pair_out,) * 9 + (row_out,) * 2,
        grid_spec=pltpu.PrefetchScalarGridSpec(
            num_scalar_prefetch=2, grid=(n_tiles,),
            in_specs=main + prev + nxt + consts,
            out_specs=(pair_spec,) * 9 + (row_spec,) * 2),
        compiler_params=_params(1),
        name="rwkv_prep",
    )(first, last, *([proj] * 12), rp["mu_r"], rp["mu_k"], rp["mu_v"], rp["mu_z"], rp["k_k"], rp["k_a"],
      rp["r_k"], rp["w0"], rp["a0"], rp["w_up"], rp["a_up"], rp["g_up"], rp["ones_bd"])


def _bf(x):
    return x.astype(BF16)


def _mm(a, b):
    return jnp.dot(_bf(a), _bf(b), preferred_element_type=F32)


def _mm_nt(a, b):
    return lax.dot_general(_bf(a), _bf(b), (((1,), (1,)), ((), ())), preferred_element_type=F32)


def _mm_tn(a, b):
    return jnp.dot(_bf(a.T), _bf(b), preferred_element_type=F32)


def _scan_blocks(units):
    c = SCAN_CHUNK
    two = 2 * c
    t_i = lax.broadcasted_iota(jnp.int32, (c, c), 0)
    s_i = lax.broadcasted_iota(jnp.int32, (c, c), 1)
    lane = lax.broadcasted_iota(jnp.int32, (c, LANES), 1)
    head0 = lane < RWKV_HEAD
    row2 = lax.broadcasted_iota(jnp.int32, (two, two), 0)
    col2 = lax.broadcasted_iota(jnp.int32, (two, two), 1)
    same_head = (row2 < c) == (col2 < c)
    eye = jnp.where(row2 == col2, 1.0, 0.0)
    rev = [u[7] for u in units]
    tri = {d: jnp.where((s_i >= t_i) if d else (s_i <= t_i), 1.0, 0.0).astype(BF16) for d in set(rev)}
    strict = {d: same_head & ((col2 > row2) if d else (col2 < row2)) for d in set(rev)}
    incl = {d: same_head & ((col2 >= row2) if d else (col2 <= row2)) for d in set(rev)}

    def stack(x):
        return jnp.concatenate([jnp.where(head0, x, 0.0), jnp.where(head0, 0.0, x)], axis=0)

    def each(fn, *cols):
        return [fn(*args) for args in zip(*cols)]

    s, r, v, kk, lw, b, kd = ([u[i] for u in units] for i in range(7))
    hi = each(_bf, lw)
    mid = each(lambda x, h: _bf(x - h.astype(F32)), lw, hi)
    lo = each(lambda x, h, m: _bf(x - h.astype(F32) - m.astype(F32)), lw, hi, mid)
    parts = each(lambda d, h, m, l: jnp.dot(tri[d], jnp.concatenate([h, m, l], axis=1),
                                            preferred_element_type=F32), rev, hi, mid, lo)
    cum = each(lambda x: x[:, 0:LANES] + x[:, LANES:2 * LANES] + x[:, 2 * LANES:3 * LANES], parts)
    total = each(lambda d, x: x[0:1, :] if d else x[c - 1:c, :], rev, cum)
    a_s = each(lambda x, cm, l: stack(-x * jnp.exp(cm - l)), kk, cum, lw)
    r_s = each(lambda x, cm: stack(x * jnp.exp(cm)), r, cum)
    g_inv = each(lambda cm: jnp.exp(-cm), cum)
    g_end = each(lambda tt, cm: jnp.exp(tt - cm), total, cum)
    b_s = each(lambda x, g: stack(x * g), b, g_inv)
    k_s = each(lambda x, g: stack(x * g), kd, g_inv)
    be_s = each(lambda x, g: stack(x * g), b, g_end)
    ke_s = each(lambda x, g: stack(x * g), kd, g_end)
    v_s = each(stack, v)

    a_ab = each(lambda d, x, y: jnp.where(strict[d], _mm_nt(x, y), 0.0), rev, a_s, b_s)
    a_ak = each(lambda d, x, y: jnp.where(strict[d], _mm_nt(x, y), 0.0), rev, a_s, k_s)
    a_rb = each(lambda d, x, y: jnp.where(incl[d], _mm_nt(x, y), 0.0), rev, r_s, b_s)
    a_rk = each(lambda d, x, y: jnp.where(incl[d], _mm_nt(x, y), 0.0), rev, r_s, k_s)

    inv = each(lambda x: eye + x, a_ab)
    power = a_ab
    w = each(_mm, a_ak, v_s)
    for _ in range(int(np.log2(c)) - 1):
        power = each(_mm, power, power)
        inv = each(lambda i, pw: i + _mm(pw, i), inv, power)

    p = each(_mm, inv, a_s)
    q = each(_mm, inv, w)
    qv = each(lambda x, y: jnp.concatenate([x, y], axis=0), q, v_s)
    g_mat = each(lambda x, y, z: x + _mm(y, z), r_s, a_rb, p)
    y0 = each(lambda x, y, z: _mm(jnp.concatenate([x, y], axis=1), z), a_rb, a_rk, qv)
    m_low = each(_mm_tn, be_s, p)
    n_t = each(lambda x, y, z: _mm_tn(x, jnp.concatenate([y, z], axis=0)), qv, be_s, ke_s)

    y = each(lambda g, st, z: _mm_nt(g, st) + z, g_mat, s, y0)
    s_new = each(lambda tt, st, m, n: jnp.exp(tt) * st + _mm_nt(st, m) + n, total, s, m_low, n_t)
    return s_new, each(lambda x: x[0:c, :] + x[c:two, :], y)


def _rwkv_scan_kernel(r_ref, v_ref, kk_ref, lwf_ref, lwb_ref, bf_ref, bb_ref, kdf_ref, kdb_ref,
                      bonus_ref, gate_ref, lnw_ref, lnb_ref, ones_ref, sf0_ref, sb0_ref,
                      o_ref, sf_o, sb_o, yf, yb, *, unroll):
    n_pairs, t = r_ref.shape[0], r_ref.shape[1]
    nc = t // SCAN_CHUNK
    sf_o[...] = sf0_ref[...]
    sb_o[...] = sb0_ref[...]

    def body(ci, carry):
        units, sinks = [], []
        for p in range(n_pairs):
            for reverse, lw_ref, b_ref, kd_ref, s_ref, y_ref in ((False, lwf_ref, bf_ref, kdf_ref, sf_o, yf),
                                                                (True, lwb_ref, bb_ref, kdb_ref, sb_o, yb)):
                cc = (nc - 1 - ci) if reverse else ci
                rows = pl.ds(pl.multiple_of(cc * SCAN_CHUNK, SCAN_CHUNK), SCAN_CHUNK)
                units.append((s_ref[p], r_ref[p, rows, :], v_ref[p, rows, :], kk_ref[p, rows, :],
                              lw_ref[p, rows, :], b_ref[p, rows, :], kd_ref[p, rows, :], reverse))
                sinks.append((s_ref, y_ref, p, rows))
        s_new, y = _scan_blocks(units)
        for (s_ref, y_ref, p, rows), s_val, y_val in zip(sinks, s_new, y):
            s_ref[p] = s_val
            y_ref[p, rows, :] = y_val
        return carry

    lax.fori_loop(0, nc, body, 0, unroll=unroll)

    ones_bd = ones_ref[...]
    for p in range(n_pairs):
        cols = slice(p * LANES, (p + 1) * LANES)
        y = yf[p] + yb[p]
        mean = jnp.dot(y, ones_bd, precision=HIGHEST, preferred_element_type=F32) * (1.0 / RWKV_HEAD)
        dev = y - mean
        var = jnp.dot(dev * dev, ones_bd, precision=HIGHEST, preferred_element_type=F32) * (1.0 / RWKV_HEAD)
        yn = dev * lax.rsqrt(var + LNX_EPS) * lnw_ref[:, cols] + lnb_ref[:, cols]
        o_ref[:, cols] = ((yn + bonus_ref[:, cols]) * gate_ref[:, cols]).astype(o_ref.dtype)


def _rwkv_scan(prep, lnx_w, lnx_b, ones_bd, s0f, s0b, n_seq, t, row_base, pairs_per_step, unroll):
    r, v, kk, lwf, lwb, bf, bb, kdf, kdb, bonus, gate = prep
    base = row_base // t
    pp = pairs_per_step
    width = pp * LANES
    pair_spec = pl.BlockSpec((pp, t, LANES), lambda s, p: (p, base + s, 0))
    row_spec = pl.BlockSpec((t, width), lambda s, p: (base + s, p))
    vec_spec = pl.BlockSpec((1, width), lambda s, p: (0, p))
    s_spec = pl.BlockSpec((None, pp, LANES, LANES), lambda s, p: (s, p, 0, 0))
    s_shape = jax.ShapeDtypeStruct((n_seq, RWKV_PAIRS, LANES, LANES), F32)
    return pl.pallas_call(
        functools.partial(_rwkv_scan_kernel, unroll=unroll),
        out_shape=(jax.ShapeDtypeStruct((n_seq * t, RWKV_WIDTH), BF16), s_shape, s_shape),
        grid=(n_seq, RWKV_PAIRS // pp),
        in_specs=[pair_spec] * 9 + [row_spec, row_spec, vec_spec, vec_spec,
                                    pl.BlockSpec((LANES, LANES), lambda s, p: (0, 0)), s_spec, s_spec],
        out_specs=(pl.BlockSpec((t, width), lambda s, p: (s, p)), s_spec, s_spec),
        scratch_shapes=[pltpu.VMEM((pp, t, LANES), F32), pltpu.VMEM((pp, t, LANES), F32)],
        compiler_params=_params(2),
        name="rwkv_scan",
    )(r, v, kk, lwf, lwb, bf, bb, kdf, kdb, bonus, gate, lnx_w, lnx_b, ones_bd, s0f, s0b)


def _state_to_blockdiag(s):
    n = s.shape[0]
    h = s.reshape(n, RWKV_PAIRS, 2, RWKV_HEAD, RWKV_HEAD)
    z = jnp.zeros_like(h[:, :, 0])
    top = jnp.concatenate([h[:, :, 0], z], axis=-1)
    bot = jnp.concatenate([z, h[:, :, 1]], axis=-1)
    return jnp.concatenate([top, bot], axis=-2)


def _blockdiag_to_state(h):
    n = h.shape[0]
    h0 = h[:, :, :RWKV_HEAD, :RWKV_HEAD]
    h1 = h[:, :, RWKV_HEAD:, RWKV_HEAD:]
    return jnp.stack([h0, h1], axis=2).reshape(n, 2 * RWKV_PAIRS, RWKV_HEAD, RWKV_HEAD)


def _out_proj_kernel(tm_ref, attc_ref, attl_ref, rwc_ref, rwl_ref, sg_ref, w_ref, x_ref, g_ref, o_ref, *, n_ctx_tiles):
    def project(att_ref, rw_ref):
        acc = jnp.dot(att_ref[...], w_ref[0:ATT_WIDTH, :], preferred_element_type=F32)
        acc += jnp.dot(rw_ref[...], w_ref[ATT_WIDTH:ATT_WIDTH + RWKV_WIDTH, :], preferred_element_type=F32)
        acc += jnp.dot(sg_ref[...], w_ref[ATT_WIDTH + RWKV_WIDTH:, :], preferred_element_type=F32)
        o_ref[...] = x_ref[...] + g_ref[0] * acc

    is_ctx = pl.program_id(0) < n_ctx_tiles

    @pl.when(is_ctx)
    def _():
        project(attc_ref, rwc_ref)

    @pl.when(jnp.logical_not(is_ctx))
    def _():
        project(attl_ref, rwl_ref)


def _out_proj(att, rw, sg, w_bf, x, mod48, lay):
    n = x.shape[0]
    tm, tn = 512, 512
    tile_mod = jnp.asarray(_tile_mod_rows(lay, tm))
    nct = lay.n_ctx * lay.t_ctx // tm
    nlt = lay.n_lat * lay.t_lat // tm

    def ctx_rows(i, j, t):
        return (jnp.minimum(i, nct - 1), 0)

    def lat_rows(i, j, t):
        return (jnp.clip(i - nct, 0, nlt - 1), 0)

    return pl.pallas_call(
        functools.partial(_out_proj_kernel, n_ctx_tiles=nct),
        out_shape=jax.ShapeDtypeStruct((n, D_MODEL), F32),
        grid_spec=pltpu.PrefetchScalarGridSpec(
            num_scalar_prefetch=1, grid=(n // tm, D_MODEL // tn),
            in_specs=[pl.BlockSpec((tm, ATT_WIDTH), ctx_rows),
                      pl.BlockSpec((tm, ATT_WIDTH), lat_rows),
                      pl.BlockSpec((tm, RWKV_WIDTH), ctx_rows),
                      pl.BlockSpec((tm, RWKV_WIDTH), lat_rows),
                      pl.BlockSpec((tm, SGU_WIDTH), lambda i, j, t: (i, 0)),
                      pl.BlockSpec((D_MODEL, tn), lambda i, j, t: (0, j)),
                      pl.BlockSpec((tm, tn), lambda i, j, t: (i, j)),
                      pl.BlockSpec((1, 1, tn), lambda i, j, t: (t[i] * 6 + 2, 0, j))],
            out_specs=pl.BlockSpec((tm, tn), lambda i, j, t: (i, j))),
        compiler_params=_params(2),
        name="out_proj",
    )(tile_mod, att[0], att[1], rw[0], rw[1], sg, w_bf, x, mod48)


def _moe_up_kernel(te_ref, tx_ref, tv_ref, tf_ref, x_ref, wg_ref, wl_ref, bg_ref, bl_ref, o_ref, wg_bf, wl_bf):
    i = pl.program_id(1)

    @pl.when(tf_ref[i] == 1)
    def _():
        wg_bf[...] = wg_ref[...].astype(BF16)
        wl_bf[...] = wl_ref[...].astype(BF16)

    @pl.when(tv_ref[i] == 1)
    def _():
        x = x_ref[...]
        z_glu = jnp.dot(x, wg_bf[...], preferred_element_type=F32) + bg_ref[...]
        z_lin = jnp.dot(x, wl_bf[...], preferred_element_type=F32) + bl_ref[...]
        z_glu = jnp.minimum(z_glu, SWIGLU_LIMIT)
        z_lin = jnp.clip(z_lin, -SWIGLU_LIMIT, SWIGLU_LIMIT)
        o_ref[...] = (z_glu * jax.nn.sigmoid(SWIGLU_ALPHA * z_glu) * (z_lin + 1.0)).astype(o_ref.dtype)

    @pl.when(tv_ref[i] == 0)
    def _():
        o_ref[...] = jnp.zeros_like(o_ref)


def _moe_down_kernel(te_ref, tx_ref, tv_ref, tf_ref, a_ref, w_ref, b_ref, o_ref, w_bf):
    i = pl.program_id(1)

    @pl.when(tf_ref[i] == 1)
    def _():
        w_bf[...] = w_ref[...].astype(BF16)

    @pl.when(tv_ref[i] == 1)
    def _():
        y = jnp.dot(a_ref[...], w_bf[...], preferred_element_type=F32) + b_ref[...]
        o_ref[...] = y.astype(o_ref.dtype)

    @pl.when(tv_ref[i] == 0)
    def _():
        o_ref[...] = jnp.zeros_like(o_ref)


def _moe_experts(xs, tiles, w1, b1, w2, b2, layer):
    te, tx, tv, tf = tiles
    r = xs.shape[0]
    n_tiles = r // MOE_TM
    nj1 = EXPERT_FF // MOE_TN1
    act = pl.pallas_call(
        _moe_up_kernel,
        out_shape=jax.ShapeDtypeStruct((r, EXPERT_FF), BF16),
        grid_spec=pltpu.PrefetchScalarGridSpec(
            num_scalar_prefetch=4, grid=(nj1, n_tiles),
            in_specs=[pl.BlockSpec((MOE_TM, D_MODEL), lambda j, i, te, tx, tv, tf: (tx[i], 0)),
                      pl.BlockSpec((None, None, D_MODEL, MOE_TN1), lambda j, i, te, tx, tv, tf: (layer, te[i], 0, j)),
                      pl.BlockSpec((None, None, D_MODEL, MOE_TN1),
                                   lambda j, i, te, tx, tv, tf: (layer, te[i], 0, nj1 + j)),
                      pl.BlockSpec((None, 1, MOE_TN1), lambda j, i, te, tx, tv, tf: (layer * N_EXPERTS + te[i], 0, j)),
                      pl.BlockSpec((None, 1, MOE_TN1),
                                   lambda j, i, te, tx, tv, tf: (layer * N_EXPERTS + te[i], 0, nj1 + j))],
            out_specs=pl.BlockSpec((MOE_TM, MOE_TN1), lambda j, i, te, tx, tv, tf: (i, j)),
            scratch_shapes=[pltpu.VMEM((D_MODEL, MOE_TN1), BF16), pltpu.VMEM((D_MODEL, MOE_TN1), BF16)]),
        compiler_params=_params(2, MOE_VMEM_LIMIT),
        name="moe_up",
    )(te, tx, tv, tf, xs, w1, w1, b1, b1)
    nj2 = D_MODEL // MOE_TN2
    return pl.pallas_call(
        _moe_down_kernel,
        out_shape=jax.ShapeDtypeStruct((r, D_MODEL), F32),
        grid_spec=pltpu.PrefetchScalarGridSpec(
            num_scalar_prefetch=4, grid=(nj2, n_tiles),
            in_specs=[pl.BlockSpec((MOE_TM, EXPERT_FF), lambda j, i, te, tx, tv, tf: (tx[i], 0)),
                      pl.BlockSpec((None, None, EXPERT_FF, MOE_TN2), lambda j, i, te, tx, tv, tf: (layer, te[i], 0, j)),
                      pl.BlockSpec((None, 1, MOE_TN2), lambda j, i, te, tx, tv, tf: (layer * N_EXPERTS + te[i], 0, j))],
            out_specs=pl.BlockSpec((MOE_TM, MOE_TN2), lambda j, i, te, tx, tv, tf: (i, j)),
            scratch_shapes=[pltpu.VMEM((EXPERT_FF, MOE_TN2), BF16)]),
        compiler_params=_params(2, MOE_VMEM_LIMIT),
        name="moe_down",
    )(te, tx, tv, tf, act, w2, b2)


def _route(sel, idx4):
    n = sel.shape[0]
    r_max = n * TOP_K + N_EXPERTS * MOE_TM
    n_tiles = r_max // MOE_TM
    sel_i = sel.astype(jnp.int32)
    counts = jnp.sum(sel_i, axis=0)
    tiles_per = (counts + MOE_TM - 1) // MOE_TM
    tile_end = jnp.cumsum(tiles_per)
    offsets = (tile_end - tiles_per) * MOE_TM
    pos_all = offsets[None, :] + jnp.cumsum(sel_i, axis=0) - sel_i
    experts = jnp.arange(N_EXPERTS, dtype=jnp.int32)
    pos4 = jnp.sum(jnp.where(idx4[:, :, None] == experts[None, None, :], pos_all[:, None, :], 0), axis=-1)

    n_used = tile_end[-1]
    tile_ids = jnp.arange(n_tiles, dtype=jnp.int32)
    tx = jnp.minimum(tile_ids, n_used - 1)
    te = jnp.minimum(jnp.sum((tile_end[None, :] <= tx[:, None]).astype(jnp.int32), axis=1), N_EXPERTS - 1)
    tv = (tile_ids < n_used).astype(jnp.int32)
    tf = jnp.concatenate([jnp.ones((1,), jnp.int32), (te[1:] != te[:-1]).astype(jnp.int32)])

    toks = jnp.repeat(jnp.arange(n, dtype=jnp.int32), TOP_K)
    src = jnp.zeros((r_max,), jnp.int32).at[pos4.reshape(-1)].set(toks, unique_indices=True)
    return src, pos4, (te.astype(jnp.int32), tx, tv, tf)


COMBINE_ROWS = TOP_K * ROW_TILE


def _expert_mix(pos_ref, x_ref, gate_ref, g_ref, ys_hbm, buf, sem):
    base = pl.program_id(0) * COMBINE_ROWS

    def row_copy(j):
        return pltpu.make_async_copy(ys_hbm.at[pl.ds(pos_ref[base + j], 1)], buf.at[pl.ds(j, 1)], sem)

    def issue(j, carry):
        row_copy(j).start()
        return carry

    def drain(j, carry):
        row_copy(j).wait()
        return carry

    lax.fori_loop(0, COMBINE_ROWS, issue, 0, unroll=8)
    lax.fori_loop(0, COMBINE_ROWS, drain, 0, unroll=8)
    y = gate_ref[:, 0:1] * buf[0:ROW_TILE, :]
    for k in range(1, TOP_K):
        y += gate_ref[:, k:k + 1] * buf[k * ROW_TILE:(k + 1) * ROW_TILE, :]
    return x_ref[...] + g_ref[0] * y


def _resid_kernel(tm_ref, pos_ref, x_ref, gate_ref, g_ref, ys_hbm, o_ref, buf, sem):
    o_ref[...] = _expert_mix(pos_ref, x_ref, gate_ref, g_ref, ys_hbm, buf, sem)


def _resid_norm_kernel(tm_ref, pos_ref, x_ref, gate_ref, g_ref, fn_ref, ys_hbm, o_ref, buf, sem):
    x = _expert_mix(pos_ref, x_ref, gate_ref, g_ref, ys_hbm, buf, sem)
    o_ref[...] = x * lax.rsqrt(jnp.mean(x * x, axis=-1, keepdims=True) + NORM_EPS) * fn_ref[...]


def _gated_residual(x, out_sorted, pos4, gates, mod48, lay, final_norm=None):
    n = x.shape[0]
    nt = n // ROW_TILE
    tile_mod = jnp.asarray(_tile_mod_rows(lay, ROW_TILE))
    pos_flat = pos4.reshape(nt, ROW_TILE, TOP_K).transpose(0, 2, 1).reshape(-1)
    tile = pl.BlockSpec((ROW_TILE, D_MODEL), lambda i, t, q: (i, 0))
    in_specs = [tile, pl.BlockSpec((ROW_TILE, LANES), lambda i, t, q: (i, 0)),
                pl.BlockSpec((1, 1, D_MODEL), lambda i, t, q: (t[i] * 6 + 5, 0, 0))]
    args = [tile_mod, pos_flat, x, gates, mod48]
    kern = _resid_kernel
    if final_norm is not None:
        in_specs.append(pl.BlockSpec((1, D_MODEL), lambda i, t, q: (0, 0)))
        args.append(final_norm)
        kern = _resid_norm_kernel
    in_specs.append(pl.BlockSpec(memory_space=pl.ANY))
    args.append(out_sorted)
    return pl.pallas_call(
        kern,
        out_shape=jax.ShapeDtypeStruct((n, D_MODEL), F32),
        grid_spec=pltpu.PrefetchScalarGridSpec(
            num_scalar_prefetch=2, grid=(nt,), in_specs=in_specs, out_specs=tile,
            scratch_shapes=[pltpu.VMEM((COMBINE_ROWS, D_MODEL), F32), pltpu.SemaphoreType.DMA(())]),
        compiler_params=_params(1),
        name="gated_residual",
    )(*args)


def _block_ones():
    idx = np.arange(LANES) // RWKV_HEAD
    return jnp.asarray((idx[:, None] == idx[None, :]).astype(np.float32))


def _layer_weights(l, w_in, w_out, rwkv_mu, w0, w_up, a0, a_up, g_up, k_k, k_a, r_k, sgu_w, sgu_b, w_router, b_router):
    w_sgu = w_in[l][None, :, SGU_IN_START:].astype(BF16)
    mu = rwkv_mu[l]
    zeros_lora = jnp.zeros((DECAY_LORA, RWKV_WIDTH), F32)

    def blockdiag(up):
        return jnp.concatenate([jnp.concatenate([up[0], zeros_lora], axis=1),
                                jnp.concatenate([zeros_lora, up[1]], axis=1)], axis=0).astype(BF16)

    rp = {
        "mu_r": mu[None, 0:RWKV_WIDTH], "mu_k": mu[None, RWKV_WIDTH:2 * RWKV_WIDTH],
        "mu_v": mu[None, 2 * RWKV_WIDTH:3 * RWKV_WIDTH],
        "mu_z": jnp.pad(mu[3 * RWKV_WIDTH:], (0, Z_WIDTH - (RWKV_IN - 3 * RWKV_WIDTH)))[None],
        "k_k": k_k[l][None], "k_a": k_a[l][None], "r_k": r_k[l].reshape(1, RWKV_WIDTH),
        "w0": w0[l], "a0": a0[l], "w_up": blockdiag(w_up[l]), "a_up": blockdiag(a_up[l]),
        "g_up": jnp.pad(g_up[l], ((0, 2 * LANES - GATE_LORA), (0, 0))).astype(BF16),
        "ones_bd": _block_ones(),
    }
    wr = jnp.pad(w_router[l], ((0, 0), (0, LANES - N_EXPERTS)))
    br = jnp.pad(b_router[l], (0, LANES - N_EXPERTS))[None]
    return {"w_sgu": w_sgu, "w_out": w_out[l].astype(BF16), "rp": rp,
            "sgu_w": sgu_w[l].astype(BF16), "sgu_bt": sgu_b[l].T, "router": (wr, br)}


def _layer(x, l, lay, mod48, lw, p, cache_k, cache_v, h0_lat, rope):
    n_ctx_rows = lay.n_ctx * lay.t_ctx
    h = _norm_mod(x, p["norm_mix"][l][None], mod48, 0, lay)
    tm = 1024 if x.shape[0] % 1024 == 0 else 512
    proj = _matmul(h, p["w_in"], l, P_WIDTH, tm, 512)
    proj_sgu = _matmul(h, lw["w_sgu"], 0, 2 * SGU_WIDTH, tm, 512)

    qn, kn = p["q_norm"][l][None], p["k_norm"][l][None]
    att_ctx, k_ctx = _attention_ctx(proj, qn, kn, lay)
    att_lat = _attention_lat(proj, cache_k, cache_v, rope[0], rope[1], qn, kn, l, lay)
    v_ctx = proj[:n_ctx_rows, COL_V:COL_V + KV_WIDTH]

    prep = _rwkv_prep(proj, lw["rp"], lay)
    lnw, lnb, ones_bd = p["lnx_w"][l][None], p["lnx_b"][l][None], lw["rp"]["ones_bd"]
    zeros_h = jnp.zeros((lay.n_ctx, RWKV_PAIRS, LANES, LANES), F32)
    rw_ctx, hf_ctx, hb_ctx = _rwkv_scan(prep, lnw, lnb, ones_bd, zeros_h, zeros_h, lay.n_ctx, lay.t_ctx, 0,
                                        pairs_per_step=SCAN_PAIRS_CTX, unroll=1)
    rw_lat, _, _ = _rwkv_scan(prep, lnw, lnb, ones_bd, h0_lat[0], h0_lat[1], lay.n_lat, lay.t_lat, n_ctx_rows,
                              pairs_per_step=SCAN_PAIRS_LAT, unroll=1)

    sg = _sgu(proj_sgu, lw["sgu_w"], lw["sgu_bt"], p["sgu_norm"][l][None])
    x = _out_proj((att_ctx, att_lat), (rw_ctx, rw_lat), sg, lw["w_out"], x, mod48, lay)

    h2, gates, idx, sel = _norm_mod(x, p["norm_ffn"][l][None], mod48, 3, lay, router=lw["router"])
    src, pos4, tiles = _route(sel[:, :N_EXPERTS], idx[:, :TOP_K])
    xs = jnp.take(h2, src, axis=0)
    out_sorted = _moe_experts(xs, tiles, p["w_exp1"], p["b_exp1r"], p["w_exp2"], p["b_exp2r"], l)
    final = p["final_norm"][None] if l == DEPTH - 1 else None
    x = _gated_residual(x, out_sorted, pos4, gates, mod48, lay, final)
    return x, k_ctx, v_ctx, hf_ctx, hb_ctx


def _forward(lay, x_prompt, x_sample, cache_k, cache_v, state_wkv_fwd, state_wkv_bwd, c, c_ctx, p):
    n_ctx_rows = lay.n_ctx * lay.t_ctx
    x = jnp.concatenate([x_prompt.reshape(n_ctx_rows, D_MODEL), x_sample.reshape(lay.n_lat * lay.t_lat, D_MODEL)])
    cond8 = jnp.zeros((8, D_MODEL), F32).at[0].set(c_ctx).at[1:1 + lay.n_lat].set(c)
    rope = _rope_tables(lay.t_lat)
    ck = cache_k.reshape(lay.n_lat, DEPTH, lay.past, KV_WIDTH)
    cv = cache_v.reshape(lay.n_lat, DEPTH, lay.past, KV_WIDTH)
    p = dict(p)
    p["b_exp1r"] = p["b_exp1"].reshape(DEPTH * N_EXPERTS, 1, 2 * EXPERT_FF)
    p["b_exp2r"] = p["b_exp2"].reshape(DEPTH * N_EXPERTS, 1, D_MODEL)
    ks, vs, sfs, sbs = [], [], [], []
    for l in range(DEPTH):
        lw = _layer_weights(l, p["w_in"], p["w_out"], p["rwkv_mu"], p["w0"], p["w_up"], p["a0"], p["a_up"],
                            p["g_up"], p["k_k"], p["k_a"], p["r_k"], p["sgu_w"], p["sgu_b"],
                            p["w_router"], p["b_router"])
        mod = _modulation(cond8, p["w_mod"], p["b_mod"], l)
        mod48 = mod.reshape(8 * 6, 1, D_MODEL)
        h0_lat = (_state_to_blockdiag(state_wkv_fwd[:, l]), _state_to_blockdiag(state_wkv_bwd[:, l]))
        x, k_l, v_l, hf, hb = _layer(x, l, lay, mod48, lw, p, ck, cv, h0_lat, rope)
        ks.append(k_l.reshape(lay.n_ctx, lay.t_ctx, ATT_KV_HEADS, HEAD_DIM))
        vs.append(v_l.reshape(lay.n_ctx, lay.t_ctx, ATT_KV_HEADS, HEAD_DIM))
        sfs.append(_blockdiag_to_state(hf))
        sbs.append(_blockdiag_to_state(hb))
    y_prompt = x[:n_ctx_rows].reshape(x_prompt.shape)
    y_sample = x[n_ctx_rows:].reshape(x_sample.shape)
    return (y_prompt, y_sample, jnp.stack(ks, axis=1), jnp.stack(vs, axis=1),
            jnp.stack(sfs, axis=1), jnp.stack(sbs, axis=1))


def kernel(x_prompt, x_sample, cache_k, cache_v, state_wkv_fwd, state_wkv_bwd, c, c_ctx, norm_mix, norm_ffn, w_mod, b_mod, w_in, w_out, q_norm, k_norm, rwkv_mu, w0, w_up, a0, a_up, g_up, k_k, k_a, r_k, lnx_w, lnx_b, sgu_norm, sgu_w, sgu_b, w_router, b_router, w_exp1, b_exp1, w_exp2, b_exp2, final_norm):
    lay = Layout(n_ctx=x_prompt.shape[0], t_ctx=x_prompt.shape[1], n_lat=x_sample.shape[0],
                 t_lat=x_sample.shape[1], past=cache_k.shape[2])
    p = dict(norm_mix=norm_mix, norm_ffn=norm_ffn, w_mod=w_mod, b_mod=b_mod, w_in=w_in, w_out=w_out,
             q_norm=q_norm, k_norm=k_norm, rwkv_mu=rwkv_mu, w0=w0, w_up=w_up, a0=a0, a_up=a_up, g_up=g_up,
             k_k=k_k, k_a=k_a, r_k=r_k, lnx_w=lnx_w, lnx_b=lnx_b, sgu_norm=sgu_norm, sgu_w=sgu_w, sgu_b=sgu_b,
             w_router=w_router, b_router=b_router, w_exp1=w_exp1, b_exp1=b_exp1, w_exp2=w_exp2, b_exp2=b_exp2,
             final_norm=final_norm)
    return _forward(lay, x_prompt, x_sample, cache_k, cache_v, state_wkv_fwd, state_wkv_bwd, c, c_ctx, p)
```

```python
import collections
import functools

import numpy as np
import jax
import jax.numpy as jnp
from jax import lax
from jax.experimental import pallas as pl
from jax.experimental.pallas import tpu as pltpu

F32 = jnp.float32
BF16 = jnp.bfloat16
HIGHEST = lax.Precision.HIGHEST

D_MODEL = 4096
DEPTH = 2
GRID_W = 64
HEAD_DIM = 128
ATT_WIDTH = 2048
ATT_KV_HEADS = 4
ATT_GROUP = 4
KV_WIDTH = 512
ROPE_THETA = 10000.0
ROPE_FREQS = 32
RWKV_WIDTH = 1024
RWKV_HEAD = 64
RWKV_PAIRS = 8
DECAY_LORA = 64
ICLR_LORA = 64
GATE_LORA = 160
RWKV_IN = 3 * RWKV_WIDTH + 2 * DECAY_LORA + 2 * ICLR_LORA + GATE_LORA
SGU_WIDTH = 1024
SGU_GROUPS = 8
CHUNK = 128
N_EXPERTS = 32
TOP_K = 4
EXPERT_FF = 2048
SWIGLU_LIMIT = 7.0
SWIGLU_ALPHA = 1.702
NORM_EPS = 1e-6
LNX_EPS = 64e-5
L2_EPS = 1e-12

LANES = 128
ROW_TILE = 256
SCAN_CHUNK = 64
SCAN_PAIRS_CTX = 4
SCAN_PAIRS_LAT = 2
VMEM_LIMIT = 48 * 1024 * 1024

COL_Q, COL_K, COL_V, COL_R, COL_RK, COL_RV, COL_Z = 0, 2048, 2560, 3072, 4096, 5120, 6144
Z_WIDTH = 512
P_WIDTH = COL_Z + Z_WIDTH
SGU_IN_START = ATT_WIDTH + 2 * KV_WIDTH + RWKV_IN
COL_U, COL_SV = 0, 1024

MOE_TM = 256
MOE_TN1 = 512
MOE_TN2 = 1024
MOE_VMEM_LIMIT = 56 * 1024 * 1024

Layout = collections.namedtuple("Layout", "n_ctx t_ctx n_lat t_lat past")


def _params(n_axes, vmem_limit=VMEM_LIMIT):
    return pltpu.CompilerParams(dimension_semantics=("arbitrary",) * n_axes, vmem_limit_bytes=vmem_limit)


def _rows(lay):
    return lay.n_ctx * lay.t_ctx + lay.n_lat * lay.t_lat


def _tile_mod_rows(lay, tile):
    n_ctx_rows = lay.n_ctx * lay.t_ctx
    starts = np.arange(0, _rows(lay), tile)
    return np.where(starts < n_ctx_rows, 0, 1 + (starts - n_ctx_rows) // lay.t_lat).astype(np.int32)


def _mod_kernel(c_ref, w_ref, b_ref, o_ref):
    c = c_ref[...]
    a = (c * jax.nn.sigmoid(c)).astype(BF16)
    o_ref[...] = jnp.dot(a, w_ref[...].astype(BF16), preferred_element_type=F32) + b_ref[...]


def _modulation(cond8, w_mod, b_mod, layer):
    tn = 512
    n = w_mod.shape[-1]
    return pl.pallas_call(
        _mod_kernel,
        out_shape=jax.ShapeDtypeStruct((8, n), F32),
        grid=(n // tn,),
        in_specs=[
            pl.BlockSpec((8, D_MODEL), lambda j: (0, 0)),
            pl.BlockSpec((None, D_MODEL, tn), lambda j: (layer, 0, j)),
            pl.BlockSpec((None, 1, tn), lambda j: (layer, 0, j)),
        ],
        out_specs=pl.BlockSpec((8, tn), lambda j: (0, j)),
        compiler_params=_params(1),
        name="modulation",
    )(cond8, w_mod, b_mod.reshape(DEPTH, 1, n))


def _modulated_norm(x_ref, g_ref, sh_ref, sc_ref):
    x = x_ref[...]
    y = x * lax.rsqrt(jnp.mean(x * x, axis=-1, keepdims=True) + NORM_EPS) * g_ref[...]
    return y * (1.0 + sc_ref[0]) + sh_ref[0]


def _norm_mod_kernel(tm_ref, x_ref, g_ref, sh_ref, sc_ref, o_ref):
    o_ref[...] = _modulated_norm(x_ref, g_ref, sh_ref, sc_ref).astype(o_ref.dtype)


def _norm_router_kernel(tm_ref, x_ref, g_ref, sh_ref, sc_ref, wr_ref, br_ref, o_ref, gate_ref, idx_ref, sel_ref):
    h = _modulated_norm(x_ref, g_ref, sh_ref, sc_ref)
    bits = lax.bitcast_convert_type(h.astype(BF16).astype(F32), jnp.uint32)
    o_ref[...] = (bits[:, :D_MODEL // 2] >> 16) | bits[:, D_MODEL // 2:]
    logits = jnp.dot(h, wr_ref[...], precision=HIGHEST, preferred_element_type=F32) + br_ref[...]
    lane = lax.broadcasted_iota(jnp.int32, logits.shape, 1)
    neg = jnp.float32(-jnp.inf)
    work = jnp.where(lane < N_EXPERTS, logits, neg)
    tops, firsts = [], []
    sel = jnp.zeros_like(logits)
    for _ in range(TOP_K):
        m = jnp.max(work, axis=-1, keepdims=True)
        first = jnp.min(jnp.where(work == m, lane, LANES), axis=-1, keepdims=True)
        hot = lane == first
        tops.append(m)
        firsts.append(first)
        sel = jnp.where(hot, 1.0, sel)
        work = jnp.where(hot, neg, work)
    exps = [jnp.exp(t - tops[0]) for t in tops]
    denom = exps[0] + exps[1] + exps[2] + exps[3]
    gates = jnp.zeros_like(logits)
    idx = jnp.zeros(logits.shape, jnp.int32)
    for k in range(TOP_K):
        gates = jnp.where(lane == k, exps[k] / denom, gates)
        idx = jnp.where(lane == k, firsts[k], idx)
    gate_ref[...] = gates
    idx_ref[...] = idx
    sel_ref[...] = sel


def _norm_mod(x, gain, mod48, which_shift, lay, router=None):
    n = x.shape[0]
    tile_mod = jnp.asarray(_tile_mod_rows(lay, ROW_TILE))
    in_specs = [
        pl.BlockSpec((ROW_TILE, D_MODEL), lambda i, tm: (i, 0)),
        pl.BlockSpec((1, D_MODEL), lambda i, tm: (0, 0)),
        pl.BlockSpec((1, 1, D_MODEL), lambda i, tm: (tm[i] * 6 + which_shift, 0, 0)),
        pl.BlockSpec((1, 1, D_MODEL), lambda i, tm: (tm[i] * 6 + which_shift + 1, 0, 0)),
    ]
    h_spec = pl.BlockSpec((ROW_TILE, D_MODEL), lambda i, tm: (i, 0))
    if router is None:
        return pl.pallas_call(
            _norm_mod_kernel,
            out_shape=jax.ShapeDtypeStruct((n, D_MODEL), BF16),
            grid_spec=pltpu.PrefetchScalarGridSpec(
                num_scalar_prefetch=1, grid=(n // ROW_TILE,), in_specs=in_specs, out_specs=h_spec),
            compiler_params=_params(1),
            name="norm_mod",
        )(tile_mod, x, gain, mod48, mod48)
    wr, br = router
    lane_spec = pl.BlockSpec((ROW_TILE, LANES), lambda i, tm: (i, 0))
    return pl.pallas_call(
        _norm_router_kernel,
        out_shape=(jax.ShapeDtypeStruct((n, D_MODEL // 2), jnp.uint32),
                   jax.ShapeDtypeStruct((n, LANES), F32),
                   jax.ShapeDtypeStruct((n, LANES), jnp.int32),
                   jax.ShapeDtypeStruct((n, LANES), F32)),
        grid_spec=pltpu.PrefetchScalarGridSpec(
            num_scalar_prefetch=1, grid=(n // ROW_TILE,),
            in_specs=in_specs + [pl.BlockSpec((D_MODEL, LANES), lambda i, tm: (0, 0)),
                                 pl.BlockSpec((1, LANES), lambda i, tm: (0, 0))],
            out_specs=(pl.BlockSpec((ROW_TILE, D_MODEL // 2), lambda i, tm: (i, 0)),
                       lane_spec, lane_spec, lane_spec)),
        compiler_params=_params(1),
        name="norm_router",
    )(tile_mod, x, gain, mod48, mod48, wr, br)


def _mm_kernel(a_ref, w_ref, o_ref):
    o_ref[...] = jnp.dot(a_ref[...], w_ref[...].astype(BF16), preferred_element_type=F32)


def _matmul(a, w, layer, n, tm, tn):
    m, k = a.shape
    return pl.pallas_call(
        _mm_kernel,
        out_shape=jax.ShapeDtypeStruct((m, n), F32),
        grid=(m // tm, n // tn),
        in_specs=[pl.BlockSpec((tm, k), lambda i, j: (i, 0)),
                  pl.BlockSpec((None, k, tn), lambda i, j: (layer, 0, j))],
        out_specs=pl.BlockSpec((tm, tn), lambda i, j: (i, j)),
        compiler_params=_params(2),
        name="in_proj",
    )(a, w)


def _head_norm(x, g):
    return x * lax.rsqrt(jnp.mean(x * x, axis=-1, keepdims=True) + NORM_EPS) * g


def _rope(x, cos, sin_signed):
    lane = lax.broadcasted_iota(jnp.int32, x.shape, 1)
    partner = jnp.where((lane & 32) == 0, pltpu.roll(x, LANES - 32, 1), pltpu.roll(x, 32, 1))
    return x * cos + partner * sin_signed


def _softmax_pv(q_bf, k_bf, v_bf):
    s = lax.dot_general(q_bf, k_bf, (((1,), (1,)), ((), ())), preferred_element_type=F32) * (HEAD_DIM ** -0.5)
    e = jnp.exp(s - jnp.max(s, axis=-1, keepdims=True))
    o = jnp.dot(e.astype(BF16), v_bf, preferred_element_type=F32)
    return o / jnp.sum(e, axis=-1, keepdims=True)


def _attn_ctx_kernel(q_ref, k_ref, v_ref, qn_ref, kn_ref, o_ref, ko_ref):
    t = q_ref.shape[0]
    kn = _head_norm(k_ref[...], kn_ref[...])
    ko_ref[...] = kn
    qs = [_head_norm(q_ref[:, h * HEAD_DIM:(h + 1) * HEAD_DIM], qn_ref[...]) for h in range(ATT_GROUP)]
    q = jnp.concatenate(qs, axis=0).astype(BF16)
    o = _softmax_pv(q, kn.astype(BF16), v_ref[...].astype(BF16))
    for h in range(ATT_GROUP):
        o_ref[:, h * HEAD_DIM:(h + 1) * HEAD_DIM] = o[h * t:(h + 1) * t].astype(o_ref.dtype)


def _attention_ctx(proj, qn, kn, lay):
    t = lay.t_ctx
    rows = lay.n_ctx * t
    kcol, vcol = COL_K // HEAD_DIM, COL_V // HEAD_DIM
    return pl.pallas_call(
        _attn_ctx_kernel,
        out_shape=(jax.ShapeDtypeStruct((rows, ATT_WIDTH), BF16), jax.ShapeDtypeStruct((rows, KV_WIDTH), F32)),
        grid=(lay.n_ctx, ATT_KV_HEADS),
        in_specs=[pl.BlockSpec((t, ATT_GROUP * HEAD_DIM), lambda b, g: (b, g)),
                  pl.BlockSpec((t, HEAD_DIM), lambda b, g: (b, kcol + g)),
                  pl.BlockSpec((t, HEAD_DIM), lambda b, g: (b, vcol + g)),
                  pl.BlockSpec((1, HEAD_DIM), lambda b, g: (0, 0)),
                  pl.BlockSpec((1, HEAD_DIM), lambda b, g: (0, 0))],
        out_specs=(pl.BlockSpec((t, ATT_GROUP * HEAD_DIM), lambda b, g: (b, g)),
                   pl.BlockSpec((t, HEAD_DIM), lambda b, g: (b, g))),
        compiler_params=_params(2),
        name="attention_ctx",
    )(proj, proj, proj, qn, kn)


def _attn_lat_kernel(q_ref, k_ref, v_ref, ck_ref, cv_ref, cosq_ref, sinq_ref, cosk_ref, sink_ref,
                     qn_ref, kn_ref, o_ref, kbuf, vbuf):
    past = ck_ref.shape[0]
    tq = q_ref.shape[0]

    @pl.when(pl.program_id(2) == 0)
    def _():
        kr = _rope(_head_norm(k_ref[...], kn_ref[...]), cosk_ref[...], sink_ref[...])
        kbuf[0:past, :] = ck_ref[...].astype(BF16)
        kbuf[past:, :] = kr.astype(BF16)
        vbuf[0:past, :] = cv_ref[...].astype(BF16)
        vbuf[past:, :] = v_ref[...].astype(BF16)

    cos, sin = cosq_ref[...], sinq_ref[...]
    qs = [_rope(_head_norm(q_ref[:, h * HEAD_DIM:(h + 1) * HEAD_DIM], qn_ref[...]), cos, sin)
          for h in range(ATT_GROUP)]
    q = jnp.concatenate(qs, axis=0).astype(BF16)
    o = _softmax_pv(q, kbuf[...], vbuf[...])
    for h in range(ATT_GROUP):
        o_ref[:, h * HEAD_DIM:(h + 1) * HEAD_DIM] = o[h * tq:(h + 1) * tq].astype(o_ref.dtype)


def _attention_lat(proj, cache_k, cache_v, cos, sin, qn, kn, layer, lay):
    t, past = lay.t_lat, lay.past
    tq = ROW_TILE
    nq = t // tq
    base_q = lay.n_ctx * lay.t_ctx // tq
    base_t = lay.n_ctx * lay.t_ctx // t
    kcol, vcol = COL_K // HEAD_DIM, COL_V // HEAD_DIM
    cache_spec = pl.BlockSpec((None, None, past, HEAD_DIM), lambda b, g, i: (b, layer, 0, g))
    return pl.pallas_call(
        _attn_lat_kernel,
        out_shape=jax.ShapeDtypeStruct((lay.n_lat * t, ATT_WIDTH), BF16),
        grid=(lay.n_lat, ATT_KV_HEADS, nq),
        in_specs=[pl.BlockSpec((tq, ATT_GROUP * HEAD_DIM), lambda b, g, i: (base_q + b * nq + i, g)),
                  pl.BlockSpec((t, HEAD_DIM), lambda b, g, i: (base_t + b, kcol + g)),
                  pl.BlockSpec((t, HEAD_DIM), lambda b, g, i: (base_t + b, vcol + g)),
                  cache_spec, cache_spec,
                  pl.BlockSpec((tq, HEAD_DIM), lambda b, g, i: (i, 0)),
                  pl.BlockSpec((tq, HEAD_DIM), lambda b, g, i: (i, 0)),
                  pl.BlockSpec((t, HEAD_DIM), lambda b, g, i: (0, 0)),
                  pl.BlockSpec((t, HEAD_DIM), lambda b, g, i: (0, 0)),
                  pl.BlockSpec((1, HEAD_DIM), lambda b, g, i: (0, 0)),
                  pl.BlockSpec((1, HEAD_DIM), lambda b, g, i: (0, 0))],
        out_specs=pl.BlockSpec((tq, ATT_GROUP * HEAD_DIM), lambda b, g, i: (b * nq + i, g)),
        scratch_shapes=[pltpu.VMEM((past + t, HEAD_DIM), BF16), pltpu.VMEM((past + t, HEAD_DIM), BF16)],
        compiler_params=_params(3),
        name="attention_lat",
    )(proj, proj, proj, cache_k, cache_v, cos, sin, cos, sin, qn, kn)


def _rope_tables(t):
    pos = np.arange(t)
    inv = ROPE_THETA ** (-np.arange(ROPE_FREQS, dtype=np.float32) / ROPE_FREQS)
    ang_r = (pos // GRID_W).astype(np.float32)[:, None] * inv
    ang_c = (pos % GRID_W).astype(np.float32)[:, None] * inv
    cr, sr, cc, sc = np.cos(ang_r), np.sin(ang_r), np.cos(ang_c), np.sin(ang_c)
    cos = np.concatenate([cr, cr, cc, cc], axis=1)
    sin = np.concatenate([-sr, sr, -sc, sc], axis=1)
    return jnp.asarray(cos, F32), jnp.asarray(sin, F32)


def _sgu_kernel(u_ref, sv_ref, w_ref, bt_ref, g_ref, o_ref):
    for g in range(SGU_GROUPS):
        cols = slice(g * LANES, (g + 1) * LANES)
        sv = sv_ref[:, cols]
        svn = sv * lax.rsqrt(jnp.mean(sv * sv, axis=-1, keepdims=True) + NORM_EPS) * g_ref[:, cols]
        z = jnp.dot(w_ref[g], svn.astype(BF16), preferred_element_type=F32) + bt_ref[:, g:g + 1]
        o_ref[:, cols] = (u_ref[:, cols] * z).astype(o_ref.dtype)


def _sgu(proj, w_bf, b_t, gain):
    n = proj.shape[0]
    ucol, svcol = COL_U // SGU_WIDTH, COL_SV // SGU_WIDTH
    return pl.pallas_call(
        _sgu_kernel,
        out_shape=jax.ShapeDtypeStruct((n, SGU_WIDTH), BF16),
        grid=(n // CHUNK,),
        in_specs=[pl.BlockSpec((CHUNK, SGU_WIDTH), lambda i: (i, ucol)),
                  pl.BlockSpec((CHUNK, SGU_WIDTH), lambda i: (i, svcol)),
                  pl.BlockSpec((SGU_GROUPS, CHUNK, CHUNK), lambda i: (0, 0, 0)),
                  pl.BlockSpec((CHUNK, SGU_GROUPS), lambda i: (0, 0)),
                  pl.BlockSpec((1, SGU_WIDTH), lambda i: (0, 0))],
        out_specs=pl.BlockSpec((CHUNK, SGU_WIDTH), lambda i: (i, 0)),
        compiler_params=_params(1),
        name="sgu",
    )(proj, proj, w_bf, b_t, gain)


def _pair_sum(x, ones_bd):
    blocks = [jnp.dot(x[:, p * LANES:(p + 1) * LANES], ones_bd, precision=HIGHEST, preferred_element_type=F32)
              for p in range(x.shape[1] // LANES)]
    return jnp.concatenate(blocks, axis=1)


def _rwkv_prep_kernel(first_ref, last_ref,
                      r_ref, k_ref, v_ref, z_ref, rp_ref, kp_ref, vp_ref, zp_ref, rn_ref, kn_ref, vn_ref, zn_ref,
                      mur_ref, muk_ref, muv_ref, muz_ref, kk_ref, ka_ref, rk_ref, w0_ref, a0_ref,
                      wup_ref, aup_ref, gup_ref, ones_ref,
                      r_o, v_o, kk_o, lwf_o, lwb_o, bf_o, bb_o, kdf_o, kdb_o, bonus_o, gate_o):
    i = pl.program_id(0)
    has_prev = first_ref[i] == 0
    has_next = last_ref[i] == 0
    row = lax.broadcasted_iota(jnp.int32, (ROW_TILE, 1), 0)

    def mix(x_ref, p_ref, n_ref, mu_ref):
        x = x_ref[...]
        prev = jnp.where(row == 0, jnp.where(has_prev, p_ref[7:8, :], 0.0), pltpu.roll(x, 1, 0))
        nxt = jnp.where(row == ROW_TILE - 1, jnp.where(has_next, n_ref[0:1, :], 0.0),
                        pltpu.roll(x, ROW_TILE - 1, 0))
        return x + (0.5 * (prev + nxt) - x) * mu_ref[...]

    r = mix(r_ref, rp_ref, rn_ref, mur_ref)
    k = mix(k_ref, kp_ref, kn_ref, muk_ref)
    v = mix(v_ref, vp_ref, vn_ref, muv_ref)
    z = mix(z_ref, zp_ref, zn_ref, muz_ref)
    ones_bd = ones_ref[...]

    kk = k * kk_ref[...]
    kk = kk / jnp.maximum(jnp.sqrt(_pair_sum(kk * kk, ones_bd)), L2_EPS)

    lora_w = jnp.dot(jnp.tanh(z[:, 0:LANES]).astype(BF16), wup_ref[...], preferred_element_type=F32)
    lora_a = jnp.dot(z[:, LANES:2 * LANES].astype(BF16), aup_ref[...], preferred_element_type=F32)
    gate_o[...] = jnp.dot(jax.nn.sigmoid(z[:, 2 * LANES:4 * LANES]).astype(BF16), gup_ref[...],
                          preferred_element_type=F32)

    iclr_sum = jnp.zeros_like(k)
    for d, (lw_o, b_o, kd_o) in enumerate(((lwf_o, bf_o, kdf_o), (lwb_o, bb_o, kdb_o))):
        cols = slice(d * RWKV_WIDTH, (d + 1) * RWKV_WIDTH)
        x = -(w0_ref[d:d + 1, :] + lora_w[:, cols])
        softplus = jnp.maximum(x, 0.0) + jnp.log(1.0 + jnp.exp(-jnp.abs(x)))
        log_decay = -jnp.exp(-softplus - 0.5)
        iclr = jax.nn.sigmoid(a0_ref[d:d + 1, :] + lora_a[:, cols])
        kd = k * (1.0 + (iclr - 1.0) * ka_ref[...])
        b = kk * iclr
        iclr_sum = iclr_sum + iclr
        for p in range(RWKV_PAIRS):
            pc = slice(p * LANES, (p + 1) * LANES)
            lw_o[p] = log_decay[:, pc]
            b_o[p] = b[:, pc]
            kd_o[p] = kd[:, pc]
    for p in range(RWKV_PAIRS):
        pc = slice(p * LANES, (p + 1) * LANES)
        r_o[p] = r[:, pc]
        v_o[p] = v[:, pc]
        kk_o[p] = kk[:, pc]
    bonus_o[...] = _pair_sum(r * k * rk_ref[...] * (2.0 + (iclr_sum - 2.0) * ka_ref[...]), ones_bd) * v


def _seq_edge_flags(lay):
    starts = np.arange(0, _rows(lay), ROW_TILE)
    n_ctx_rows = lay.n_ctx * lay.t_ctx
    in_seq = np.where(starts < n_ctx_rows, starts % lay.t_ctx, (starts - n_ctx_rows) % lay.t_lat)
    seq_len = np.where(starts < n_ctx_rows, lay.t_ctx, lay.t_lat)
    first = (in_seq == 0).astype(np.int32)
    last = (in_seq + ROW_TILE == seq_len).astype(np.int32)
    return jnp.asarray(first), jnp.asarray(last)


def _rwkv_prep(proj, rp, lay):
    n = proj.shape[0]
    n_tiles = n // ROW_TILE
    first, last = _seq_edge_flags(lay)
    sub = ROW_TILE // 8
    wcols = {"r": (RWKV_WIDTH, COL_R // RWKV_WIDTH), "k": (RWKV_WIDTH, COL_RK // RWKV_WIDTH),
             "v": (RWKV_WIDTH, COL_RV // RWKV_WIDTH), "z": (Z_WIDTH, COL_Z // Z_WIDTH)}
    main = [pl.BlockSpec((ROW_TILE, w), functools.partial(lambda i, f, l, c: (i, c), c=c))
            for w, c in wcols.values()]
    prev = [pl.BlockSpec((8, w), functools.partial(lambda i, f, l, c: (jnp.maximum(i * sub - 1, 0), c), c=c))
            for w, c in wcols.values()]
    nxt = [pl.BlockSpec((8, w), functools.partial(
        lambda i, f, l, c: (jnp.minimum((i + 1) * sub, n_tiles * sub - 1), c), c=c)) for w, c in wcols.values()]

    def const(shape):
        return pl.BlockSpec(shape, lambda i, f, l: (0,) * len(shape))

    consts = [const((1, RWKV_WIDTH))] * 3 + [const((1, Z_WIDTH))] + [const((1, RWKV_WIDTH))] * 3 + \
        [const((2, RWKV_WIDTH))] * 2 + [const((LANES, 2 * RWKV_WIDTH))] * 2 + \
        [const((2 * LANES, RWKV_WIDTH)), const((LANES, LANES))]
    pair_out = jax.ShapeDtypeStruct((RWKV_PAIRS, n, LANES), F32)
    pair_spec = pl.BlockSpec((RWKV_PAIRS, ROW_TILE, LANES), lambda i, f, l: (0, i, 0))
    row_out = jax.ShapeDtypeStruct((n, RWKV_WIDTH), F32)
    row_spec = pl.BlockSpec((ROW_TILE, RWKV_WIDTH), lambda i, f, l: (i, 0))
    return pl.pallas_call(
        _rwkv_prep_kernel,
        out_shape=(pair_out,) * 9 + (row_out,) * 2,
        grid_spec=pltpu.PrefetchScalarGridSpec(
            num_scalar_prefetch=2, grid=(n_tiles,),
            in_specs=main + prev + nxt + consts,
            out_specs=(pair_spec,) * 9 + (row_spec,) * 2),
        compiler_params=_params(1),
        name="rwkv_prep",
    )(first, last, *([proj] * 12), rp["mu_r"], rp["mu_k"], rp["mu_v"], rp["mu_z"], rp["k_k"], rp["k_a"],
      rp["r_k"], rp["w0"], rp["a0"], rp["w_up"], rp["a_up"], rp["g_up"], rp["ones_bd"])


def _bf(x):
    return x.astype(BF16)


def _mm(a, b):
    return jnp.dot(_bf(a), _bf(b), preferred_element_type=F32)


def _mm_nt(a, b):
    return lax.dot_general(_bf(a), _bf(b), (((1,), (1,)), ((), ())), preferred_element_type=F32)


def _mm_tn(a, b):
    return jnp.dot(_bf(a.T), _bf(b), preferred_element_type=F32)


def _scan_blocks(units):
    c = SCAN_CHUNK
    two = 2 * c
    t_i = lax.broadcasted_iota(jnp.int32, (c, c), 0)
    s_i = lax.broadcasted_iota(jnp.int32, (c, c), 1)
    lane = lax.broadcasted_iota(jnp.int32, (c, LANES), 1)
    head0 = lane < RWKV_HEAD
    row2 = lax.broadcasted_iota(jnp.int32, (two, two), 0)
    col2 = lax.broadcasted_iota(jnp.int32, (two, two), 1)
    same_head = (row2 < c) == (col2 < c)
    eye = jnp.where(row2 == col2, 1.0, 0.0)
    rev = [u[7] for u in units]
    tri = {d: jnp.where((s_i >= t_i) if d else (s_i <= t_i), 1.0, 0.0).astype(BF16) for d in set(rev)}
    strict = {d: same_head & ((col2 > row2) if d else (col2 < row2)) for d in set(rev)}
    incl = {d: same_head & ((col2 >= row2) if d else (col2 <= row2)) for d in set(rev)}

    def stack(x):
        return jnp.concatenate([jnp.where(head0, x, 0.0), jnp.where(head0, 0.0, x)], axis=0)

    def each(fn, *cols):
        return [fn(*args) for args in zip(*cols)]

    s, r, v, kk, lw, b, kd = ([u[i] for u in units] for i in range(7))
    hi = each(_bf, lw)
    mid = each(lambda x, h: _bf(x - h.astype(F32)), lw, hi)
    lo = each(lambda x, h, m: _bf(x - h.astype(F32) - m.astype(F32)), lw, hi, mid)
    parts = each(lambda d, h, m, l: jnp.dot(tri[d], jnp.concatenate([h, m, l], axis=1),
                                            preferred_element_type=F32), rev, hi, mid, lo)
    cum = each(lambda x: x[:, 0:LANES] + x[:, LANES:2 * LANES] + x[:, 2 * LANES:3 * LANES], parts)
    total = each(lambda d, x: x[0:1, :] if d else x[c - 1:c, :], rev, cum)
    a_s = each(lambda x, cm, l: stack(-x * jnp.exp(cm - l)), kk, cum, lw)
    r_s = each(lambda x, cm: stack(x * jnp.exp(cm)), r, cum)
    g_inv = each(lambda cm: jnp.exp(-cm), cum)
    g_end = each(lambda tt, cm: jnp.exp(tt - cm), total, cum)
    b_s = each(lambda x, g: stack(x * g), b, g_inv)
    k_s = each(lambda x, g: stack(x * g), kd, g_inv)
    be_s = each(lambda x, g: stack(x * g), b, g_end)
    ke_s = each(lambda x, g: stack(x * g), kd, g_end)
    v_s = each(stack, v)

    a_ab = each(lambda d, x, y: jnp.where(strict[d], _mm_nt(x, y), 0.0), rev, a_s, b_s)
    a_ak = each(lambda d, x, y: jnp.where(strict[d], _mm_nt(x, y), 0.0), rev, a_s, k_s)
    a_rb = each(lambda d, x, y: jnp.where(incl[d], _mm_nt(x, y), 0.0), rev, r_s, b_s)
    a_rk = each(lambda d, x, y: jnp.where(incl[d], _mm_nt(x, y), 0.0), rev, r_s, k_s)

    inv = each(lambda x: eye + x, a_ab)
    power = a_ab
    w = each(_mm, a_ak, v_s)
    for _ in range(int(np.log2(c)) - 1):
        power = each(_mm, power, power)
        inv = each(lambda i, pw: i + _mm(pw, i), inv, power)

    p = each(_mm, inv, a_s)
    q = each(_mm, inv, w)
    qv = each(lambda x, y: jnp.concatenate([x, y], axis=0), q, v_s)
    g_mat = each(lambda x, y, z: x + _mm(y, z), r_s, a_rb, p)
    y0 = each(lambda x, y, z: _mm(jnp.concatenate([x, y], axis=1), z), a_rb, a_rk, qv)
    m_low = each(_mm_tn, be_s, p)
    n_t = each(lambda x, y, z: _mm_tn(x, jnp.concatenate([y, z], axis=0)), qv, be_s, ke_s)

    y = each(lambda g, st, z: _mm_nt(g, st) + z, g_mat, s, y0)
    s_new = each(lambda tt, st, m, n: jnp.exp(tt) * st + _mm_nt(st, m) + n, total, s, m_low, n_t)
    return s_new, each(lambda x: x[0:c, :] + x[c:two, :], y)


def _rwkv_scan_kernel(r_ref, v_ref, kk_ref, lwf_ref, lwb_ref, bf_ref, bb_ref, kdf_ref, kdb_ref,
                      bonus_ref, gate_ref, lnw_ref, lnb_ref, ones_ref, sf0_ref, sb0_ref,
                      o_ref, sf_o, sb_o, yf, yb, *, unroll):
    n_pairs, t = r_ref.shape[0], r_ref.shape[1]
    nc = t // SCAN_CHUNK
    sf_o[...] = sf0_ref[...]
    sb_o[...] = sb0_ref[...]

    def body(ci, carry):
        units, sinks = [], []
        for p in range(n_pairs):
            for reverse, lw_ref, b_ref, kd_ref, s_ref, y_ref in ((False, lwf_ref, bf_ref, kdf_ref, sf_o, yf),
                                                                (True, lwb_ref, bb_ref, kdb_ref, sb_o, yb)):
                cc = (nc - 1 - ci) if reverse else ci
                rows = pl.ds(pl.multiple_of(cc * SCAN_CHUNK, SCAN_CHUNK), SCAN_CHUNK)
                units.append((s_ref[p], r_ref[p, rows, :], v_ref[p, rows, :], kk_ref[p, rows, :],
                              lw_ref[p, rows, :], b_ref[p, rows, :], kd_ref[p, rows, :], reverse))
                sinks.append((s_ref, y_ref, p, rows))
        s_new, y = _scan_blocks(units)
        for (s_ref, y_ref, p, rows), s_val, y_val in zip(sinks, s_new, y):
            s_ref[p] = s_val
            y_ref[p, rows, :] = y_val
        return carry

    lax.fori_loop(0, nc, body, 0, unroll=unroll)

    ones_bd = ones_ref[...]
    for p in range(n_pairs):
        cols = slice(p * LANES, (p + 1) * LANES)
        y = yf[p] + yb[p]
        mean = jnp.dot(y, ones_bd, precision=HIGHEST, preferred_element_type=F32) * (1.0 / RWKV_HEAD)
        dev = y - mean
        var = jnp.dot(dev * dev, ones_bd, precision=HIGHEST, preferred_element_type=F32) * (1.0 / RWKV_HEAD)
        yn = dev * lax.rsqrt(var + LNX_EPS) * lnw_ref[:, cols] + lnb_ref[:, cols]
        o_ref[:, cols] = ((yn + bonus_ref[:, cols]) * gate_ref[:, cols]).astype(o_ref.dtype)


def _rwkv_scan(prep, lnx_w, lnx_b, ones_bd, s0f, s0b, n_seq, t, row_base, pairs_per_step, unroll):
    r, v, kk, lwf, lwb, bf, bb, kdf, kdb, bonus, gate = prep
    base = row_base // t
    pp = pairs_per_step
    width = pp * LANES
    pair_spec = pl.BlockSpec((pp, t, LANES), lambda s, p: (p, base + s, 0))
    row_spec = pl.BlockSpec((t, width), lambda s, p: (base + s, p))
    vec_spec = pl.BlockSpec((1, width), lambda s, p: (0, p))
    s_spec = pl.BlockSpec((None, pp, LANES, LANES), lambda s, p: (s, p, 0, 0))
    s_shape = jax.ShapeDtypeStruct((n_seq, RWKV_PAIRS, LANES, LANES), F32)
    return pl.pallas_call(
        functools.partial(_rwkv_scan_kernel, unroll=unroll),
        out_shape=(jax.ShapeDtypeStruct((n_seq * t, RWKV_WIDTH), BF16), s_shape, s_shape),
        grid=(n_seq, RWKV_PAIRS // pp),
        in_specs=[pair_spec] * 9 + [row_spec, row_spec, vec_spec, vec_spec,
                                    pl.BlockSpec((LANES, LANES), lambda s, p: (0, 0)), s_spec, s_spec],
        out_specs=(pl.BlockSpec((t, width), lambda s, p: (s, p)), s_spec, s_spec),
        scratch_shapes=[pltpu.VMEM((pp, t, LANES), F32), pltpu.VMEM((pp, t, LANES), F32)],
        compiler_params=_params(2),
        name="rwkv_scan",
    )(r, v, kk, lwf, lwb, bf, bb, kdf, kdb, bonus, gate, lnx_w, lnx_b, ones_bd, s0f, s0b)


def _state_to_blockdiag(s):
    n = s.shape[0]
    h = s.reshape(n, RWKV_PAIRS, 2, RWKV_HEAD, RWKV_HEAD)
    z = jnp.zeros_like(h[:, :, 0])
    top = jnp.concatenate([h[:, :, 0], z], axis=-1)
    bot = jnp.concatenate([z, h[:, :, 1]], axis=-1)
    return jnp.concatenate([top, bot], axis=-2)


def _blockdiag_to_state(h):
    n = h.shape[0]
    h0 = h[:, :, :RWKV_HEAD, :RWKV_HEAD]
    h1 = h[:, :, RWKV_HEAD:, RWKV_HEAD:]
    return jnp.stack([h0, h1], axis=2).reshape(n, 2 * RWKV_PAIRS, RWKV_HEAD, RWKV_HEAD)


def _out_proj_kernel(tm_ref, attc_ref, attl_ref, rwc_ref, rwl_ref, sg_ref, w_ref, x_ref, g_ref, o_ref, *, n_ctx_tiles):
    def project(att_ref, rw_ref):
        acc = jnp.dot(att_ref[...], w_ref[0:ATT_WIDTH, :], preferred_element_type=F32)
        acc += jnp.dot(rw_ref[...], w_ref[ATT_WIDTH:ATT_WIDTH + RWKV_WIDTH, :], preferred_element_type=F32)
        acc += jnp.dot(sg_ref[...], w_ref[ATT_WIDTH + RWKV_WIDTH:, :], preferred_element_type=F32)
        o_ref[...] = x_ref[...] + g_ref[0] * acc

    is_ctx = pl.program_id(0) < n_ctx_tiles

    @pl.when(is_ctx)
    def _():
        project(attc_ref, rwc_ref)

    @pl.when(jnp.logical_not(is_ctx))
    def _():
        project(attl_ref, rwl_ref)


def _out_proj(att, rw, sg, w_bf, x, mod48, lay):
    n = x.shape[0]
    tm, tn = 512, 512
    tile_mod = jnp.asarray(_tile_mod_rows(lay, tm))
    nct = lay.n_ctx * lay.t_ctx // tm
    nlt = lay.n_lat * lay.t_lat // tm

    def ctx_rows(i, j, t):
        return (jnp.minimum(i, nct - 1), 0)

    def lat_rows(i, j, t):
        return (jnp.clip(i - nct, 0, nlt - 1), 0)

    return pl.pallas_call(
        functools.partial(_out_proj_kernel, n_ctx_tiles=nct),
        out_shape=jax.ShapeDtypeStruct((n, D_MODEL), F32),
        grid_spec=pltpu.PrefetchScalarGridSpec(
            num_scalar_prefetch=1, grid=(n // tm, D_MODEL // tn),
            in_specs=[pl.BlockSpec((tm, ATT_WIDTH), ctx_rows),
                      pl.BlockSpec((tm, ATT_WIDTH), lat_rows),
                      pl.BlockSpec((tm, RWKV_WIDTH), ctx_rows),
                      pl.BlockSpec((tm, RWKV_WIDTH), lat_rows),
                      pl.BlockSpec((tm, SGU_WIDTH), lambda i, j, t: (i, 0)),
                      pl.BlockSpec((D_MODEL, tn), lambda i, j, t: (0, j)),
                      pl.BlockSpec((tm, tn), lambda i, j, t: (i, j)),
                      pl.BlockSpec((1, 1, tn), lambda i, j, t: (t[i] * 6 + 2, 0, j))],
            out_specs=pl.BlockSpec((tm, tn), lambda i, j, t: (i, j))),
        compiler_params=_params(2),
        name="out_proj",
    )(tile_mod, att[0], att[1], rw[0], rw[1], sg, w_bf, x, mod48)


def _dispatch_kernel(src_ref, tv_ref, h_hbm, o_ref, sem):
    i = pl.program_id(0)
    base = i * MOE_TM

    def row_copy(j):
        return pltpu.make_async_copy(h_hbm.at[pl.ds(src_ref[base + j], 1)], o_ref.at[pl.ds(j, 1)], sem)

    def issue(j, carry):
        row_copy(j).start()
        return carry

    def drain(j, carry):
        row_copy(j).wait()
        return carry

    @pl.when(tv_ref[i] == 1)
    def _():
        lax.fori_loop(0, MOE_TM, issue, 0, unroll=8)
        lax.fori_loop(0, MOE_TM, drain, 0, unroll=8)

    @pl.when(tv_ref[i] == 0)
    def _():
        o_ref[...] = jnp.zeros_like(o_ref)


def _dispatch(h_packed, src, tv):
    r = src.shape[0]
    half = h_packed.shape[1]
    return pl.pallas_call(
        _dispatch_kernel,
        out_shape=jax.ShapeDtypeStruct((r, half), jnp.uint32),
        grid_spec=pltpu.PrefetchScalarGridSpec(
            num_scalar_prefetch=2, grid=(r // MOE_TM,),
            in_specs=[pl.BlockSpec(memory_space=pl.ANY)],
            out_specs=pl.BlockSpec((MOE_TM, half), lambda i, s, v: (i, 0)),
            scratch_shapes=[pltpu.SemaphoreType.DMA(())]),
        compiler_params=_params(1),
        name="moe_dispatch",
    )(src, tv, h_packed)


def _unpack_bf16_pair(words):
    lo = lax.bitcast_convert_type(words << 16, F32).astype(BF16)
    hi = lax.bitcast_convert_type(words & jnp.uint32(0xFFFF0000), F32).astype(BF16)
    return lo, hi


def _stream_expert_weights(te_ref, tf_ref, tn_ref, tl_ref, w_hbm, layer, col_starts, width, stage, w_bf, sem, cnt_ref):
    j, i = pl.program_id(0), pl.program_id(1)
    nj = pl.num_programs(0)

    def copies(expert, jj, slot):
        out = []
        for part, start in enumerate(col_starts):
            cols = pl.ds(pl.multiple_of(start + jj * width, width), width)
            out.append(pltpu.make_async_copy(w_hbm.at[layer, expert, :, cols], stage.at[slot, part], sem.at[slot, part]))
        return out

    @pl.when(tf_ref[i] == 1)
    def _():
        @pl.when(jnp.logical_and(j == 0, i == 0))
        def _():
            cnt_ref[0] = 0
            for c in copies(te_ref[0], 0, 0):
                c.start()

        slot = cnt_ref[0] & 1
        for c in copies(te_ref[i], j, slot):
            c.wait()
        next_j = j + tl_ref[i]

        @pl.when(next_j < nj)
        def _():
            for c in copies(tn_ref[i], next_j, 1 - slot):
                c.start()

        for part in range(len(col_starts)):
            w_bf[part] = stage[slot, part].astype(BF16)
        cnt_ref[0] = cnt_ref[0] + 1


def _moe_up_kernel(te_ref, tx_ref, tv_ref, tf_ref, tn_ref, tl_ref, x_ref, w_hbm, bg_ref, bl_ref, o_ref,
                   stage, w_bf, sem, cnt_ref, *, layer):
    i = pl.program_id(1)
    half = D_MODEL // 2
    _stream_expert_weights(te_ref, tf_ref, tn_ref, tl_ref, w_hbm, layer, (0, EXPERT_FF), MOE_TN1,
                           stage, w_bf, sem, cnt_ref)

    @pl.when(tv_ref[i] == 1)
    def _():
        lo, hi = _unpack_bf16_pair(x_ref[...])
        z_glu = (jnp.dot(lo, w_bf[0, 0:half, :], preferred_element_type=F32)
                 + jnp.dot(hi, w_bf[0, half:, :], preferred_element_type=F32) + bg_ref[...])
        z_lin = (jnp.dot(lo, w_bf[1, 0:half, :], preferred_element_type=F32)
                 + jnp.dot(hi, w_bf[1, half:, :], preferred_element_type=F32) + bl_ref[...])
        z_glu = jnp.minimum(z_glu, SWIGLU_LIMIT)
        z_lin = jnp.clip(z_lin, -SWIGLU_LIMIT, SWIGLU_LIMIT)
        o_ref[...] = (z_glu * jax.nn.sigmoid(SWIGLU_ALPHA * z_glu) * (z_lin + 1.0)).astype(o_ref.dtype)

    @pl.when(tv_ref[i] == 0)
    def _():
        o_ref[...] = jnp.zeros_like(o_ref)


def _moe_down_kernel(te_ref, tx_ref, tv_ref, tf_ref, tn_ref, tl_ref, a_ref, w_hbm, b_ref, o_ref,
                     stage, w_bf, sem, cnt_ref, *, layer):
    i = pl.program_id(1)
    _stream_expert_weights(te_ref, tf_ref, tn_ref, tl_ref, w_hbm, layer, (0,), MOE_TN2, stage, w_bf, sem, cnt_ref)

    @pl.when(tv_ref[i] == 1)
    def _():
        y = jnp.dot(a_ref[...], w_bf[0], preferred_element_type=F32) + b_ref[...]
        o_ref[...] = y.astype(o_ref.dtype)

    @pl.when(tv_ref[i] == 0)
    def _():
        o_ref[...] = jnp.zeros_like(o_ref)


def _moe_experts(xs, tiles, w1, b1, w2, b2, layer):
    te, tx, tv, tf, tn, tl = tiles
    r = xs.shape[0]
    n_tiles = r // MOE_TM
    nj1 = EXPERT_FF // MOE_TN1
    hbm = pl.BlockSpec(memory_space=pl.ANY)
    act = pl.pallas_call(
        functools.partial(_moe_up_kernel, layer=layer),
        out_shape=jax.ShapeDtypeStruct((r, EXPERT_FF), BF16),
        grid_spec=pltpu.PrefetchScalarGridSpec(
            num_scalar_prefetch=6, grid=(nj1, n_tiles),
            in_specs=[pl.BlockSpec((MOE_TM, D_MODEL // 2), lambda j, i, te, tx, *_: (tx[i], 0)),
                      hbm,
                      pl.BlockSpec((None, 1, MOE_TN1), lambda j, i, te, *_: (layer * N_EXPERTS + te[i], 0, j)),
                      pl.BlockSpec((None, 1, MOE_TN1), lambda j, i, te, *_: (layer * N_EXPERTS + te[i], 0, nj1 + j))],
            out_specs=pl.BlockSpec((MOE_TM, MOE_TN1), lambda j, i, *_: (i, j)),
            scratch_shapes=[pltpu.VMEM((2, 2, D_MODEL, MOE_TN1), F32), pltpu.VMEM((2, D_MODEL, MOE_TN1), BF16),
                            pltpu.SemaphoreType.DMA((2, 2)), pltpu.SMEM((1,), jnp.int32)]),
        compiler_params=_params(2, MOE_VMEM_LIMIT),
        name="moe_up",
    )(te, tx, tv, tf, tn, tl, xs, w1, b1, b1)
    nj2 = D_MODEL // MOE_TN2
    return pl.pallas_call(
        functools.partial(_moe_down_kernel, layer=layer),
        out_shape=jax.ShapeDtypeStruct((r, D_MODEL), F32),
        grid_spec=pltpu.PrefetchScalarGridSpec(
            num_scalar_prefetch=6, grid=(nj2, n_tiles),
            in_specs=[pl.BlockSpec((MOE_TM, EXPERT_FF), lambda j, i, te, tx, *_: (tx[i], 0)),
                      hbm,
                      pl.BlockSpec((None, 1, MOE_TN2), lambda j, i, te, *_: (layer * N_EXPERTS + te[i], 0, j))],
            out_specs=pl.BlockSpec((MOE_TM, MOE_TN2), lambda j, i, *_: (i, j)),
            scratch_shapes=[pltpu.VMEM((2, 1, EXPERT_FF, MOE_TN2), F32), pltpu.VMEM((1, EXPERT_FF, MOE_TN2), BF16),
                            pltpu.SemaphoreType.DMA((2, 1)), pltpu.SMEM((1,), jnp.int32)]),
        compiler_params=_params(2, MOE_VMEM_LIMIT),
        name="moe_down",
    )(te, tx, tv, tf, tn, tl, act, w2, b2)


def _route(sel, idx4):
    n = sel.shape[0]
    r_max = n * TOP_K + N_EXPERTS * MOE_TM
    n_tiles = r_max // MOE_TM
    sel_i = sel.astype(jnp.int32)
    counts = jnp.sum(sel_i, axis=0)
    tiles_per = (counts + MOE_TM - 1) // MOE_TM
    tile_end = jnp.cumsum(tiles_per)
    offsets = (tile_end - tiles_per) * MOE_TM
    pos_all = offsets[None, :] + jnp.cumsum(sel_i, axis=0) - sel_i
    experts = jnp.arange(N_EXPERTS, dtype=jnp.int32)
    pos4 = jnp.sum(jnp.where(idx4[:, :, None] == experts[None, None, :], pos_all[:, None, :], 0), axis=-1)

    n_used = tile_end[-1]
    tile_ids = jnp.arange(n_tiles, dtype=jnp.int32)
    tx = jnp.minimum(tile_ids, n_used - 1)
    te = jnp.minimum(jnp.sum((tile_end[None, :] <= tx[:, None]).astype(jnp.int32), axis=1), N_EXPERTS - 1)
    tv = (tile_ids < n_used).astype(jnp.int32)
    tf = jnp.concatenate([jnp.ones((1,), jnp.int32), (te[1:] != te[:-1]).astype(jnp.int32)])

    te = te.astype(jnp.int32)
    later_first = (tile_ids[None, :] > tile_ids[:, None]) & (tf[None, :] == 1) & (tv[None, :] == 1)
    next_first = jnp.min(jnp.where(later_first, tile_ids[None, :], n_tiles), axis=1)
    tl = (next_first == n_tiles).astype(jnp.int32)
    tn = jnp.sum(jnp.where(tile_ids[None, :] == jnp.where(tl == 1, 0, next_first)[:, None], te[None, :], 0), axis=1)

    toks = jnp.repeat(jnp.arange(n, dtype=jnp.int32), TOP_K)
    src = jnp.zeros((r_max,), jnp.int32).at[pos4.reshape(-1)].set(toks, unique_indices=True)
    return src, pos4, (te, tx, tv, tf, tn.astype(jnp.int32), tl)


COMBINE_ROWS = TOP_K * ROW_TILE


def _expert_mix(pos_ref, x_ref, gate_ref, g_ref, ys_hbm, buf, sem):
    base = pl.program_id(0) * COMBINE_ROWS

    def row_copy(j):
        return pltpu.make_async_copy(ys_hbm.at[pl.ds(pos_ref[base + j], 1)], buf.at[pl.ds(j, 1)], sem)

    def issue(j, carry):
        row_copy(j).start()
        return carry

    def drain(j, carry):
        row_copy(j).wait()
        return carry

    lax.fori_loop(0, COMBINE_ROWS, issue, 0, unroll=8)
    lax.fori_loop(0, COMBINE_ROWS, drain, 0, unroll=8)
    y = gate_ref[:, 0:1] * buf[0:ROW_TILE, :]
    for k in range(1, TOP_K):
        y += gate_ref[:, k:k + 1] * buf[k * ROW_TILE:(k + 1) * ROW_TILE, :]
    return x_ref[...] + g_ref[0] * y


def _resid_kernel(tm_ref, pos_ref, x_ref, gate_ref, g_ref, ys_hbm, o_ref, buf, sem):
    o_ref[...] = _expert_mix(pos_ref, x_ref, gate_ref, g_ref, ys_hbm, buf, sem)


def _resid_norm_kernel(tm_ref, pos_ref, x_ref, gate_ref, g_ref, fn_ref, ys_hbm, o_ref, buf, sem):
    x = _expert_mix(pos_ref, x_ref, gate_ref, g_ref, ys_hbm, buf, sem)
    o_ref[...] = x * lax.rsqrt(jnp.mean(x * x, axis=-1, keepdims=True) + NORM_EPS) * fn_ref[...]


def _gated_residual(x, out_sorted, pos4, gates, mod48, lay, final_norm=None):
    n = x.shape[0]
    nt = n // ROW_TILE
    tile_mod = jnp.asarray(_tile_mod_rows(lay, ROW_TILE))
    pos_flat = pos4.reshape(nt, ROW_TILE, TOP_K).transpose(0, 2, 1).reshape(-1)
    tile = pl.BlockSpec((ROW_TILE, D_MODEL), lambda i, t, q: (i, 0))
    in_specs = [tile, pl.BlockSpec((ROW_TILE, LANES), lambda i, t, q: (i, 0)),
                pl.BlockSpec((1, 1, D_MODEL), lambda i, t, q: (t[i] * 6 + 5, 0, 0))]
    args = [tile_mod, pos_flat, x, gates, mod48]
    kern = _resid_kernel
    if final_norm is not None:
        in_specs.append(pl.BlockSpec((1, D_MODEL), lambda i, t, q: (0, 0)))
        args.append(final_norm)
        kern = _resid_norm_kernel
    in_specs.append(pl.BlockSpec(memory_space=pl.ANY))
    args.append(out_sorted)
    return pl.pallas_call(
        kern,
        out_shape=jax.ShapeDtypeStruct((n, D_MODEL), F32),
        grid_spec=pltpu.PrefetchScalarGridSpec(
            num_scalar_prefetch=2, grid=(nt,), in_specs=in_specs, out_specs=tile,
            scratch_shapes=[pltpu.VMEM((COMBINE_ROWS, D_MODEL), F32), pltpu.SemaphoreType.DMA(())]),
        compiler_params=_params(1),
        name="gated_residual",
    )(*args)


def _block_ones():
    idx = np.arange(LANES) // RWKV_HEAD
    return jnp.asarray((idx[:, None] == idx[None, :]).astype(np.float32))


def _layer_weights(l, w_in, w_out, rwkv_mu, w0, w_up, a0, a_up, g_up, k_k, k_a, r_k, sgu_w, sgu_b, w_router, b_router):
    w_sgu = w_in[l][None, :, SGU_IN_START:].astype(BF16)
    mu = rwkv_mu[l]
    zeros_lora = jnp.zeros((DECAY_LORA, RWKV_WIDTH), F32)

    def blockdiag(up):
        return jnp.concatenate([jnp.concatenate([up[0], zeros_lora], axis=1),
                                jnp.concatenate([zeros_lora, up[1]], axis=1)], axis=0).astype(BF16)

    rp = {
        "mu_r": mu[None, 0:RWKV_WIDTH], "mu_k": mu[None, RWKV_WIDTH:2 * RWKV_WIDTH],
        "mu_v": mu[None, 2 * RWKV_WIDTH:3 * RWKV_WIDTH],
        "mu_z": jnp.pad(mu[3 * RWKV_WIDTH:], (0, Z_WIDTH - (RWKV_IN - 3 * RWKV_WIDTH)))[None],
        "k_k": k_k[l][None], "k_a": k_a[l][None], "r_k": r_k[l].reshape(1, RWKV_WIDTH),
        "w0": w0[l], "a0": a0[l], "w_up": blockdiag(w_up[l]), "a_up": blockdiag(a_up[l]),
        "g_up": jnp.pad(g_up[l], ((0, 2 * LANES - GATE_LORA), (0, 0))).astype(BF16),
        "ones_bd": _block_ones(),
    }
    wr = jnp.pad(w_router[l], ((0, 0), (0, LANES - N_EXPERTS)))
    br = jnp.pad(b_router[l], (0, LANES - N_EXPERTS))[None]
    return {"w_sgu": w_sgu, "w_out": w_out[l].astype(BF16), "rp": rp,
            "sgu_w": sgu_w[l].astype(BF16), "sgu_bt": sgu_b[l].T, "router": (wr, br)}


def _layer(x, l, lay, mod48, lw, p, cache_k, cache_v, h0_lat, rope):
    n_ctx_rows = lay.n_ctx * lay.t_ctx
    h = _norm_mod(x, p["norm_mix"][l][None], mod48, 0, lay)
    tm = 1024 if x.shape[0] % 1024 == 0 else 512
    proj = _matmul(h, p["w_in"], l, P_WIDTH, tm, 512)
    proj_sgu = _matmul(h, lw["w_sgu"], 0, 2 * SGU_WIDTH, tm, 512)

    qn, kn = p["q_norm"][l][None], p["k_norm"][l][None]
    att_ctx, k_ctx = _attention_ctx(proj, qn, kn, lay)
    att_lat = _attention_lat(proj, cache_k, cache_v, rope[0], rope[1], qn, kn, l, lay)
    v_ctx = proj[:n_ctx_rows, COL_V:COL_V + KV_WIDTH]

    prep = _rwkv_prep(proj, lw["rp"], lay)
    lnw, lnb, ones_bd = p["lnx_w"][l][None], p["lnx_b"][l][None], lw["rp"]["ones_bd"]
    zeros_h = jnp.zeros((lay.n_ctx, RWKV_PAIRS, LANES, LANES), F32)
    rw_ctx, hf_ctx, hb_ctx = _rwkv_scan(prep, lnw, lnb, ones_bd, zeros_h, zeros_h, lay.n_ctx, lay.t_ctx, 0,
                                        pairs_per_step=SCAN_PAIRS_CTX, unroll=1)
    rw_lat, _, _ = _rwkv_scan(prep, lnw, lnb, ones_bd, h0_lat[0], h0_lat[1], lay.n_lat, lay.t_lat, n_ctx_rows,
                              pairs_per_step=SCAN_PAIRS_LAT, unroll=1)

    sg = _sgu(proj_sgu, lw["sgu_w"], lw["sgu_bt"], p["sgu_norm"][l][None])
    x = _out_proj((att_ctx, att_lat), (rw_ctx, rw_lat), sg, lw["w_out"], x, mod48, lay)

    h2, gates, idx, sel = _norm_mod(x, p["norm_ffn"][l][None], mod48, 3, lay, router=lw["router"])
    src, pos4, tiles = _route(sel[:, :N_EXPERTS], idx[:, :TOP_K])
    xs = _dispatch(h2, src, tiles[2])
    out_sorted = _moe_experts(xs, tiles, p["w_exp1"], p["b_exp1r"], p["w_exp2"], p["b_exp2r"], l)
    final = p["final_norm"][None] if l == DEPTH - 1 else None
    x = _gated_residual(x, out_sorted, pos4, gates, mod48, lay, final)
    return x, k_ctx, v_ctx, hf_ctx, hb_ctx


def _forward(lay, x_prompt, x_sample, cache_k, cache_v, state_wkv_fwd, state_wkv_bwd, c, c_ctx, p):
    n_ctx_rows = lay.n_ctx * lay.t_ctx
    x = jnp.concatenate([x_prompt.reshape(n_ctx_rows, D_MODEL), x_sample.reshape(lay.n_lat * lay.t_lat, D_MODEL)])
    cond8 = jnp.zeros((8, D_MODEL), F32).at[0].set(c_ctx).at[1:1 + lay.n_lat].set(c)
    rope = _rope_tables(lay.t_lat)
    ck = cache_k.reshape(lay.n_lat, DEPTH, lay.past, KV_WIDTH)
    cv = cache_v.reshape(lay.n_lat, DEPTH, lay.past, KV_WIDTH)
    p = dict(p)
    p["b_exp1r"] = p["b_exp1"].reshape(DEPTH * N_EXPERTS, 1, 2 * EXPERT_FF)
    p["b_exp2r"] = p["b_exp2"].reshape(DEPTH * N_EXPERTS, 1, D_MODEL)
    ks, vs, sfs, sbs = [], [], [], []
    for l in range(DEPTH):
        lw = _layer_weights(l, p["w_in"], p["w_out"], p["rwkv_mu"], p["w0"], p["w_up"], p["a0"], p["a_up"],
                            p["g_up"], p["k_k"], p["k_a"], p["r_k"], p["sgu_w"], p["sgu_b"],
                            p["w_router"], p["b_router"])
        mod = _modulation(cond8, p["w_mod"], p["b_mod"], l)
        mod48 = mod.reshape(8 * 6, 1, D_MODEL)
        h0_lat = (_state_to_blockdiag(state_wkv_fwd[:, l]), _state_to_blockdiag(state_wkv_bwd[:, l]))
        x, k_l, v_l, hf, hb = _layer(x, l, lay, mod48, lw, p, ck, cv, h0_lat, rope)
        ks.append(k_l.reshape(lay.n_ctx, lay.t_ctx, ATT_KV_HEADS, HEAD_DIM))
        vs.append(v_l.reshape(lay.n_ctx, lay.t_ctx, ATT_KV_HEADS, HEAD_DIM))
        sfs.append(_blockdiag_to_state(hf))
        sbs.append(_blockdiag_to_state(hb))
    y_prompt = x[:n_ctx_rows].reshape(x_prompt.shape)
    y_sample = x[n_ctx_rows:].reshape(x_sample.shape)
    return (y_prompt, y_sample, jnp.stack(ks, axis=1), jnp.stack(vs, axis=1),
            jnp.stack(sfs, axis=1), jnp.stack(sbs, axis=1))


def kernel(x_prompt, x_sample, cache_k, cache_v, state_wkv_fwd, state_wkv_bwd, c, c_ctx, norm_mix, norm_ffn, w_mod, b_mod, w_in, w_out, q_norm, k_norm, rwkv_mu, w0, w_up, a0, a_up, g_up, k_k, k_a, r_k, lnx_w, lnx_b, sgu_norm, sgu_w, sgu_b, w_router, b_router, w_exp1, b_exp1, w_exp2, b_exp2, final_norm):
    lay = Layout(n_ctx=x_prompt.shape[0], t_ctx=x_prompt.shape[1], n_lat=x_sample.shape[0],
                 t_lat=x_sample.shape[1], past=cache_k.shape[2])
    p = dict(norm_mix=norm_mix, norm_ffn=norm_ffn, w_mod=w_mod, b_mod=b_mod, w_in=w_in, w_out=w_out,
             q_norm=q_norm, k_norm=k_norm, rwkv_mu=rwkv_mu, w0=w0, w_up=w_up, a0=a0, a_up=a_up, g_up=g_up,
             k_k=k_k, k_a=k_a, r_k=r_k, lnx_w=lnx_w, lnx_b=lnx_b, sgu_norm=sgu_norm, sgu_w=sgu_w, sgu_b=sgu_b,
             w_router=w_router, b_router=b_router, w_exp1=w_exp1, b_exp1=b_exp1, w_exp2=w_exp2, b_exp2=b_exp2,
             final_norm=final_norm)
    return _forward(lay, x_prompt, x_sample, cache_k, cache_v, state_wkv_fwd, state_wkv_bwd, c, c_ctx, p)
```

```python
import collections
import functools

import numpy as np
import jax
import jax.numpy as jnp
from jax import lax
from jax.experimental import pallas as pl
from jax.experimental.pallas import tpu as pltpu

F32 = jnp.float32
BF16 = jnp.bfloat16
HIGHEST = lax.Precision.HIGHEST

D_MODEL = 4096
DEPTH = 2
GRID_W = 64
HEAD_DIM = 128
ATT_WIDTH = 2048
ATT_KV_HEADS = 4
ATT_GROUP = 4
KV_WIDTH = 512
ROPE_THETA = 10000.0
ROPE_FREQS = 32
RWKV_WIDTH = 1024
RWKV_HEAD = 64
RWKV_PAIRS = 8
DECAY_LORA = 64
ICLR_LORA = 64
GATE_LORA = 160
RWKV_IN = 3 * RWKV_WIDTH + 2 * DECAY_LORA + 2 * ICLR_LORA + GATE_LORA
SGU_WIDTH = 1024
SGU_GROUPS = 8
CHUNK = 128
N_EXPERTS = 32
TOP_K = 4
EXPERT_FF = 2048
SWIGLU_LIMIT = 7.0
SWIGLU_ALPHA = 1.702
NORM_EPS = 1e-6
LNX_EPS = 64e-5
L2_EPS = 1e-12

LANES = 128
ROW_TILE = 256
SCAN_CHUNK = 64
SCAN_PAIRS_CTX = 4
SCAN_PAIRS_LAT = 2
VMEM_LIMIT = 48 * 1024 * 1024

COL_Q, COL_K, COL_V, COL_R, COL_RK, COL_RV, COL_Z = 0, 2048, 2560, 3072, 4096, 5120, 6144
Z_WIDTH = 512
P_WIDTH = COL_Z + Z_WIDTH
SGU_IN_START = ATT_WIDTH + 2 * KV_WIDTH + RWKV_IN
COL_U, COL_SV = 0, 1024

MOE_TM = 256
MOE_TN1 = 512
MOE_TN2 = 1024
MOE_VMEM_LIMIT = 56 * 1024 * 1024

Layout = collections.namedtuple("Layout", "n_ctx t_ctx n_lat t_lat past")


def _params(n_axes, vmem_limit=VMEM_LIMIT):
    return pltpu.CompilerParams(dimension_semantics=("arbitrary",) * n_axes, vmem_limit_bytes=vmem_limit)


def _rows(lay):
    return lay.n_ctx * lay.t_ctx + lay.n_lat * lay.t_lat


def _tile_mod_rows(lay, tile):
    n_ctx_rows = lay.n_ctx * lay.t_ctx
    starts = np.arange(0, _rows(lay), tile)
    return np.where(starts < n_ctx_rows, 0, 1 + (starts - n_ctx_rows) // lay.t_lat).astype(np.int32)


def _mod_kernel(c_ref, w_ref, b_ref, o_ref):
    c = c_ref[...]
    a = (c * jax.nn.sigmoid(c)).astype(BF16)
    o_ref[...] = jnp.dot(a, w_ref[...].astype(BF16), preferred_element_type=F32) + b_ref[...]


def _modulation(cond8, w_mod, b_mod, layer):
    tn = 512
    n = w_mod.shape[-1]
    return pl.pallas_call(
        _mod_kernel,
        out_shape=jax.ShapeDtypeStruct((8, n), F32),
        grid=(n // tn,),
        in_specs=[
            pl.BlockSpec((8, D_MODEL), lambda j: (0, 0)),
            pl.BlockSpec((None, D_MODEL, tn), lambda j: (layer, 0, j)),
            pl.BlockSpec((None, 1, tn), lambda j: (layer, 0, j)),
        ],
        out_specs=pl.BlockSpec((8, tn), lambda j: (0, j)),
        compiler_params=_params(1),
        name="modulation",
    )(cond8, w_mod, b_mod.reshape(DEPTH, 1, n))


def _modulated_norm(x_ref, g_ref, sh_ref, sc_ref):
    x = x_ref[...]
    y = x * lax.rsqrt(jnp.mean(x * x, axis=-1, keepdims=True) + NORM_EPS) * g_ref[...]
    return y * (1.0 + sc_ref[0]) + sh_ref[0]


def _norm_mod_kernel(tm_ref, x_ref, g_ref, sh_ref, sc_ref, o_ref):
    o_ref[...] = _modulated_norm(x_ref, g_ref, sh_ref, sc_ref).astype(o_ref.dtype)


def _norm_router_kernel(tm_ref, x_ref, g_ref, sh_ref, sc_ref, wr_ref, br_ref, o_ref, gate_ref, idx_ref, sel_ref):
    h = _modulated_norm(x_ref, g_ref, sh_ref, sc_ref)
    bits = lax.bitcast_convert_type(h.astype(BF16).astype(F32), jnp.uint32)
    o_ref[...] = (bits[:, :D_MODEL // 2] >> 16) | bits[:, D_MODEL // 2:]
    logits = jnp.dot(h, wr_ref[...], precision=HIGHEST, preferred_element_type=F32) + br_ref[...]
    lane = lax.broadcasted_iota(jnp.int32, logits.shape, 1)
    neg = jnp.float32(-jnp.inf)
    work = jnp.where(lane < N_EXPERTS, logits, neg)
    tops, firsts = [], []
    sel = jnp.zeros_like(logits)
    for _ in range(TOP_K):
        m = jnp.max(work, axis=-1, keepdims=True)
        first = jnp.min(jnp.where(work == m, lane, LANES), axis=-1, keepdims=True)
        hot = lane == first
        tops.append(m)
        firsts.append(first)
        sel = jnp.where(hot, 1.0, sel)
        work = jnp.where(hot, neg, work)
    exps = [jnp.exp(t - tops[0]) for t in tops]
    denom = exps[0] + exps[1] + exps[2] + exps[3]
    gates = jnp.zeros_like(logits)
    idx = jnp.zeros(logits.shape, jnp.int32)
    for k in range(TOP_K):
        gates = jnp.where(lane == k, exps[k] / denom, gates)
        idx = jnp.where(lane == k, firsts[k], idx)
    gate_ref[...] = gates
    idx_ref[...] = idx
    sel_ref[...] = sel


def _norm_mod(x, gain, mod48, which_shift, lay, router=None):
    n = x.shape[0]
    tile_mod = jnp.asarray(_tile_mod_rows(lay, ROW_TILE))
    in_specs = [
        pl.BlockSpec((ROW_TILE, D_MODEL), lambda i, tm: (i, 0)),
        pl.BlockSpec((1, D_MODEL), lambda i, tm: (0, 0)),
        pl.BlockSpec((1, 1, D_MODEL), lambda i, tm: (tm[i] * 6 + which_shift, 0, 0)),
        pl.BlockSpec((1, 1, D_MODEL), lambda i, tm: (tm[i] * 6 + which_shift + 1, 0, 0)),
    ]
    h_spec = pl.BlockSpec((ROW_TILE, D_MODEL), lambda i, tm: (i, 0))
    if router is None:
        return pl.pallas_call(
            _norm_mod_kernel,
            out_shape=jax.ShapeDtypeStruct((n, D_MODEL), BF16),
            grid_spec=pltpu.PrefetchScalarGridSpec(
                num_scalar_prefetch=1, grid=(n // ROW_TILE,), in_specs=in_specs, out_specs=h_spec),
            compiler_params=_params(1),
            name="norm_mod",
        )(tile_mod, x, gain, mod48, mod48)
    wr, br = router
    lane_spec = pl.BlockSpec((ROW_TILE, LANES), lambda i, tm: (i, 0))
    return pl.pallas_call(
        _norm_router_kernel,
        out_shape=(jax.ShapeDtypeStruct((n, D_MODEL // 2), jnp.uint32),
                   jax.ShapeDtypeStruct((n, LANES), F32),
                   jax.ShapeDtypeStruct((n, LANES), jnp.int32),
                   jax.ShapeDtypeStruct((n, LANES), F32)),
        grid_spec=pltpu.PrefetchScalarGridSpec(
            num_scalar_prefetch=1, grid=(n // ROW_TILE,),
            in_specs=in_specs + [pl.BlockSpec((D_MODEL, LANES), lambda i, tm: (0, 0)),
                                 pl.BlockSpec((1, LANES), lambda i, tm: (0, 0))],
            out_specs=(pl.BlockSpec((ROW_TILE, D_MODEL // 2), lambda i, tm: (i, 0)),
                       lane_spec, lane_spec, lane_spec)),
        compiler_params=_params(1),
        name="norm_router",
    )(tile_mod, x, gain, mod48, mod48, wr, br)


def _mm_kernel(a_ref, w_ref, o_ref):
    o_ref[...] = jnp.dot(a_ref[...], w_ref[...].astype(BF16), preferred_element_type=F32)


def _matmul(a, w, layer, n, tm, tn):
    m, k = a.shape
    return pl.pallas_call(
        _mm_kernel,
        out_shape=jax.ShapeDtypeStruct((m, n), F32),
        grid=(m // tm, n // tn),
        in_specs=[pl.BlockSpec((tm, k), lambda i, j: (i, 0)),
                  pl.BlockSpec((None, k, tn), lambda i, j: (layer, 0, j))],
        out_specs=pl.BlockSpec((tm, tn), lambda i, j: (i, j)),
        compiler_params=_params(2),
        name="in_proj",
    )(a, w)


def _head_norm(x, g):
    return x * lax.rsqrt(jnp.mean(x * x, axis=-1, keepdims=True) + NORM_EPS) * g


def _rope(x, cos, sin_signed):
    lane = lax.broadcasted_iota(jnp.int32, x.shape, 1)
    partner = jnp.where((lane & 32) == 0, pltpu.roll(x, LANES - 32, 1), pltpu.roll(x, 32, 1))
    return x * cos + partner * sin_signed


def _softmax_pv(q_bf, k_bf, v_bf):
    s = lax.dot_general(q_bf, k_bf, (((1,), (1,)), ((), ())), preferred_element_type=F32) * (HEAD_DIM ** -0.5)
    e = jnp.exp(s - jnp.max(s, axis=-1, keepdims=True))
    o = jnp.dot(e.astype(BF16), v_bf, preferred_element_type=F32)
    return o / jnp.sum(e, axis=-1, keepdims=True)


def _attn_ctx_kernel(q_ref, k_ref, v_ref, qn_ref, kn_ref, o_ref, ko_ref):
    t = q_ref.shape[0]
    kn = _head_norm(k_ref[...], kn_ref[...])
    ko_ref[...] = kn
    qs = [_head_norm(q_ref[:, h * HEAD_DIM:(h + 1) * HEAD_DIM], qn_ref[...]) for h in range(ATT_GROUP)]
    q = jnp.concatenate(qs, axis=0).astype(BF16)
    o = _softmax_pv(q, kn.astype(BF16), v_ref[...].astype(BF16))
    for h in range(ATT_GROUP):
        o_ref[:, h * HEAD_DIM:(h + 1) * HEAD_DIM] = o[h * t:(h + 1) * t].astype(o_ref.dtype)


def _attention_ctx(proj, qn, kn, lay):
    t = lay.t_ctx
    rows = lay.n_ctx * t
    kcol, vcol = COL_K // HEAD_DIM, COL_V // HEAD_DIM
    return pl.pallas_call(
        _attn_ctx_kernel,
        out_shape=(jax.ShapeDtypeStruct((rows, ATT_WIDTH), BF16), jax.ShapeDtypeStruct((rows, KV_WIDTH), F32)),
        grid=(lay.n_ctx, ATT_KV_HEADS),
        in_specs=[pl.BlockSpec((t, ATT_GROUP * HEAD_DIM), lambda b, g: (b, g)),
                  pl.BlockSpec((t, HEAD_DIM), lambda b, g: (b, kcol + g)),
                  pl.BlockSpec((t, HEAD_DIM), lambda b, g: (b, vcol + g)),
                  pl.BlockSpec((1, HEAD_DIM), lambda b, g: (0, 0)),
                  pl.BlockSpec((1, HEAD_DIM), lambda b, g: (0, 0))],
        out_specs=(pl.BlockSpec((t, ATT_GROUP * HEAD_DIM), lambda b, g: (b, g)),
                   pl.BlockSpec((t, HEAD_DIM), lambda b, g: (b, g))),
        compiler_params=_params(2),
        name="attention_ctx",
    )(proj, proj, proj, qn, kn)


def _attn_lat_kernel(q_ref, k_ref, v_ref, ck_ref, cv_ref, cosq_ref, sinq_ref, cosk_ref, sink_ref,
                     qn_ref, kn_ref, o_ref, kbuf, vbuf):
    past = ck_ref.shape[0]
    tq = q_ref.shape[0]

    @pl.when(pl.program_id(2) == 0)
    def _():
        kr = _rope(_head_norm(k_ref[...], kn_ref[...]), cosk_ref[...], sink_ref[...])
        kbuf[0:past, :] = ck_ref[...].astype(BF16)
        kbuf[past:, :] = kr.astype(BF16)
        vbuf[0:past, :] = cv_ref[...].astype(BF16)
        vbuf[past:, :] = v_ref[...].astype(BF16)

    cos, sin = cosq_ref[...], sinq_ref[...]
    qs = [_rope(_head_norm(q_ref[:, h * HEAD_DIM:(h + 1) * HEAD_DIM], qn_ref[...]), cos, sin)
          for h in range(ATT_GROUP)]
    q = jnp.concatenate(qs, axis=0).astype(BF16)
    o = _softmax_pv(q, kbuf[...], vbuf[...])
    for h in range(ATT_GROUP):
        o_ref[:, h * HEAD_DIM:(h + 1) * HEAD_DIM] = o[h * tq:(h + 1) * tq].astype(o_ref.dtype)


def _attention_lat(proj, cache_k, cache_v, cos, sin, qn, kn, layer, lay):
    t, past = lay.t_lat, lay.past
    tq = ROW_TILE
    nq = t // tq
    base_q = lay.n_ctx * lay.t_ctx // tq
    base_t = lay.n_ctx * lay.t_ctx // t
    kcol, vcol = COL_K // HEAD_DIM, COL_V // HEAD_DIM
    cache_spec = pl.BlockSpec((None, None, past, HEAD_DIM), lambda b, g, i: (b, layer, 0, g))
    return pl.pallas_call(
        _attn_lat_kernel,
        out_shape=jax.ShapeDtypeStruct((lay.n_lat * t, ATT_WIDTH), BF16),
        grid=(lay.n_lat, ATT_KV_HEADS, nq),
        in_specs=[pl.BlockSpec((tq, ATT_GROUP * HEAD_DIM), lambda b, g, i: (base_q + b * nq + i, g)),
                  pl.BlockSpec((t, HEAD_DIM), lambda b, g, i: (base_t + b, kcol + g)),
                  pl.BlockSpec((t, HEAD_DIM), lambda b, g, i: (base_t + b, vcol + g)),
                  cache_spec, cache_spec,
                  pl.BlockSpec((tq, HEAD_DIM), lambda b, g, i: (i, 0)),
                  pl.BlockSpec((tq, HEAD_DIM), lambda b, g, i: (i, 0)),
                  pl.BlockSpec((t, HEAD_DIM), lambda b, g, i: (0, 0)),
                  pl.BlockSpec((t, HEAD_DIM), lambda b, g, i: (0, 0)),
                  pl.BlockSpec((1, HEAD_DIM), lambda b, g, i: (0, 0)),
                  pl.BlockSpec((1, HEAD_DIM), lambda b, g, i: (0, 0))],
        out_specs=pl.BlockSpec((tq, ATT_GROUP * HEAD_DIM), lambda b, g, i: (b * nq + i, g)),
        scratch_shapes=[pltpu.VMEM((past + t, HEAD_DIM), BF16), pltpu.VMEM((past + t, HEAD_DIM), BF16)],
        compiler_params=_params(3),
        name="attention_lat",
    )(proj, proj, proj, cache_k, cache_v, cos, sin, cos, sin, qn, kn)


def _rope_tables(t):
    pos = np.arange(t)
    inv = ROPE_THETA ** (-np.arange(ROPE_FREQS, dtype=np.float32) / ROPE_FREQS)
    ang_r = (pos // GRID_W).astype(np.float32)[:, None] * inv
    ang_c = (pos % GRID_W).astype(np.float32)[:, None] * inv
    cr, sr, cc, sc = np.cos(ang_r), np.sin(ang_r), np.cos(ang_c), np.sin(ang_c)
    cos = np.concatenate([cr, cr, cc, cc], axis=1)
    sin = np.concatenate([-sr, sr, -sc, sc], axis=1)
    return jnp.asarray(cos, F32), jnp.asarray(sin, F32)


def _sgu_kernel(u_ref, sv_ref, w_ref, bt_ref, g_ref, o_ref):
    for g in range(SGU_GROUPS):
        cols = slice(g * LANES, (g + 1) * LANES)
        sv = sv_ref[:, cols]
        svn = sv * lax.rsqrt(jnp.mean(sv * sv, axis=-1, keepdims=True) + NORM_EPS) * g_ref[:, cols]
        z = jnp.dot(w_ref[g], svn.astype(BF16), preferred_element_type=F32) + bt_ref[:, g:g + 1]
        o_ref[:, cols] = (u_ref[:, cols] * z).astype(o_ref.dtype)


def _sgu(proj, w_bf, b_t, gain):
    n = proj.shape[0]
    ucol, svcol = COL_U // SGU_WIDTH, COL_SV // SGU_WIDTH
    return pl.pallas_call(
        _sgu_kernel,
        out_shape=jax.ShapeDtypeStruct((n, SGU_WIDTH), BF16),
        grid=(n // CHUNK,),
        in_specs=[pl.BlockSpec((CHUNK, SGU_WIDTH), lambda i: (i, ucol)),
                  pl.BlockSpec((CHUNK, SGU_WIDTH), lambda i: (i, svcol)),
                  pl.BlockSpec((SGU_GROUPS, CHUNK, CHUNK), lambda i: (0, 0, 0)),
                  pl.BlockSpec((CHUNK, SGU_GROUPS), lambda i: (0, 0)),
                  pl.BlockSpec((1, SGU_WIDTH), lambda i: (0, 0))],
        out_specs=pl.BlockSpec((CHUNK, SGU_WIDTH), lambda i: (i, 0)),
        compiler_params=_params(1),
        name="sgu",
    )(proj, proj, w_bf, b_t, gain)


def _pair_sum(x, ones_bd):
    blocks = [jnp.dot(x[:, p * LANES:(p + 1) * LANES], ones_bd, precision=HIGHEST, preferred_element_type=F32)
              for p in range(x.shape[1] // LANES)]
    return jnp.concatenate(blocks, axis=1)


def _rwkv_prep_kernel(first_ref, last_ref,
                      r_ref, k_ref, v_ref, z_ref, rp_ref, kp_ref, vp_ref, zp_ref, rn_ref, kn_ref, vn_ref, zn_ref,
                      mur_ref, muk_ref, muv_ref, muz_ref, kk_ref, ka_ref, rk_ref, w0_ref, a0_ref,
                      wup_ref, aup_ref, gup_ref, ones_ref,
                      r_o, v_o, kk_o, lwf_o, lwb_o, bf_o, bb_o, kdf_o, kdb_o, bonus_o, gate_o):
    i = pl.program_id(0)
    has_prev = first_ref[i] == 0
    has_next = last_ref[i] == 0
    row = lax.broadcasted_iota(jnp.int32, (ROW_TILE, 1), 0)

    def mix(x_ref, p_ref, n_ref, mu_ref):
        x = x_ref[...]
        prev = jnp.where(row == 0, jnp.where(has_prev, p_ref[7:8, :], 0.0), pltpu.roll(x, 1, 0))
        nxt = jnp.where(row == ROW_TILE - 1, jnp.where(has_next, n_ref[0:1, :], 0.0),
                        pltpu.roll(x, ROW_TILE - 1, 0))
        return x + (0.5 * (prev + nxt) - x) * mu_ref[...]

    r = mix(r_ref, rp_ref, rn_ref, mur_ref)
    k = mix(k_ref, kp_ref, kn_ref, muk_ref)
    v = mix(v_ref, vp_ref, vn_ref, muv_ref)
    z = mix(z_ref, zp_ref, zn_ref, muz_ref)
    ones_bd = ones_ref[...]

    kk = k * kk_ref[...]
    kk = kk / jnp.maximum(jnp.sqrt(_pair_sum(kk * kk, ones_bd)), L2_EPS)

    lora_w = jnp.dot(jnp.tanh(z[:, 0:LANES]).astype(BF16), wup_ref[...], preferred_element_type=F32)
    lora_a = jnp.dot(z[:, LANES:2 * LANES].astype(BF16), aup_ref[...], preferred_element_type=F32)
    gate_o[...] = jnp.dot(jax.nn.sigmoid(z[:, 2 * LANES:4 * LANES]).astype(BF16), gup_ref[...],
                          preferred_element_type=F32)

    iclr_sum = jnp.zeros_like(k)
    for d, (lw_o, b_o, kd_o) in enumerate(((lwf_o, bf_o, kdf_o), (lwb_o, bb_o, kdb_o))):
        cols = slice(d * RWKV_WIDTH, (d + 1) * RWKV_WIDTH)
        x = -(w0_ref[d:d + 1, :] + lora_w[:, cols])
        softplus = jnp.maximum(x, 0.0) + jnp.log(1.0 + jnp.exp(-jnp.abs(x)))
        log_decay = -jnp.exp(-softplus - 0.5)
        iclr = jax.nn.sigmoid(a0_ref[d:d + 1, :] + lora_a[:, cols])
        kd = k * (1.0 + (iclr - 1.0) * ka_ref[...])
        b = kk * iclr
        iclr_sum = iclr_sum + iclr
        for p in range(RWKV_PAIRS):
            pc = slice(p * LANES, (p + 1) * LANES)
            lw_o[p] = log_decay[:, pc]
            b_o[p] = b[:, pc]
            kd_o[p] = kd[:, pc]
    for p in range(RWKV_PAIRS):
        pc = slice(p * LANES, (p + 1) * LANES)
        r_o[p] = r[:, pc]
        v_o[p] = v[:, pc]
        kk_o[p] = kk[:, pc]
    bonus_o[...] = _pair_sum(r * k * rk_ref[...] * (2.0 + (iclr_sum - 2.0) * ka_ref[...]), ones_bd) * v


def _seq_edge_flags(lay):
    starts = np.arange(0, _rows(lay), ROW_TILE)
    n_ctx_rows = lay.n_ctx * lay.t_ctx
    in_seq = np.where(starts < n_ctx_rows, starts % lay.t_ctx, (starts - n_ctx_rows) % lay.t_lat)
    seq_len = np.where(starts < n_ctx_rows, lay.t_ctx, lay.t_lat)
    first = (in_seq == 0).astype(np.int32)
    last = (in_seq + ROW_TILE == seq_len).astype(np.int32)
    return jnp.asarray(first), jnp.asarray(last)


def _rwkv_prep(proj, rp, lay):
    n = proj.shape[0]
    n_tiles = n // ROW_TILE
    first, last = _seq_edge_flags(lay)
    sub = ROW_TILE // 8
    wcols = {"r": (RWKV_WIDTH, COL_R // RWKV_WIDTH), "k": (RWKV_WIDTH, COL_RK // RWKV_WIDTH),
             "v": (RWKV_WIDTH, COL_RV // RWKV_WIDTH), "z": (Z_WIDTH, COL_Z // Z_WIDTH)}
    main = [pl.BlockSpec((ROW_TILE, w), functools.partial(lambda i, f, l, c: (i, c), c=c))
            for w, c in wcols.values()]
    prev = [pl.BlockSpec((8, w), functools.partial(lambda i, f, l, c: (jnp.maximum(i * sub - 1, 0), c), c=c))
            for w, c in wcols.values()]
    nxt = [pl.BlockSpec((8, w), functools.partial(
        lambda i, f, l, c: (jnp.minimum((i + 1) * sub, n_tiles * sub - 1), c), c=c)) for w, c in wcols.values()]

    def const(shape):
        return pl.BlockSpec(shape, lambda i, f, l: (0,) * len(shape))

    consts = [const((1, RWKV_WIDTH))] * 3 + [const((1, Z_WIDTH))] + [const((1, RWKV_WIDTH))] * 3 + \
        [const((2, RWKV_WIDTH))] * 2 + [const((LANES, 2 * RWKV_WIDTH))] * 2 + \
        [const((2 * LANES, RWKV_WIDTH)), const((LANES, LANES))]
    pair_out = jax.ShapeDtypeStruct((RWKV_PAIRS, n, LANES), F32)
    pair_spec = pl.BlockSpec((RWKV_PAIRS, ROW_TILE, LANES), lambda i, f, l: (0, i, 0))
    row_out = jax.ShapeDtypeStruct((n, RWKV_WIDTH), F32)
    row_spec = pl.BlockSpec((ROW_TILE, RWKV_WIDTH), lambda i, f, l: (i, 0))
    return pl.pallas_call(
        _rwkv_prep_kernel,
        out_shape=(pair_out,) * 9 + (row_out,) * 2,
        grid_spec=pltpu.PrefetchScalarGridSpec(
            num_scalar_prefetch=2, grid=(n_tiles,),
            in_specs=main + prev + nxt + consts,
            out_specs=(pair_spec,) * 9 + (row_spec,) * 2),
        compiler_params=_params(1),
        name="rwkv_prep",
    )(first, last, *([proj] * 12), rp["mu_r"], rp["mu_k"], rp["mu_v"], rp["mu_z"], rp["k_k"], rp["k_a"],
      rp["r_k"], rp["w0"], rp["a0"], rp["w_up"], rp["a_up"], rp["g_up"], rp["ones_bd"])


def _bf(x):
    return x.astype(BF16)


def _mm(a, b):
    return jnp.dot(_bf(a), _bf(b), preferred_element_type=F32)


def _mm_nt(a, b):
    return lax.dot_general(_bf(a), _bf(b), (((1,), (1,)), ((), ())), preferred_element_type=F32)


def _mm_tn(a, b):
    return jnp.dot(_bf(a.T), _bf(b), preferred_element_type=F32)


def _scan_blocks(units):
    c = SCAN_CHUNK
    two = 2 * c
    t_i = lax.broadcasted_iota(jnp.int32, (c, c), 0)
    s_i = lax.broadcasted_iota(jnp.int32, (c, c), 1)
    lane = lax.broadcasted_iota(jnp.int32, (c, LANES), 1)
    head0 = lane < RWKV_HEAD
    row2 = lax.broadcasted_iota(jnp.int32, (two, two), 0)
    col2 = lax.broadcasted_iota(jnp.int32, (two, two), 1)
    same_head = (row2 < c) == (col2 < c)
    eye = jnp.where(row2 == col2, 1.0, 0.0)
    rev = [u[7] for u in units]
    tri = {d: jnp.where((s_i >= t_i) if d else (s_i <= t_i), 1.0, 0.0).astype(BF16) for d in set(rev)}
    strict = {d: same_head & ((col2 > row2) if d else (col2 < row2)) for d in set(rev)}
    incl = {d: same_head & ((col2 >= row2) if d else (col2 <= row2)) for d in set(rev)}

    def stack(x):
        return jnp.concatenate([jnp.where(head0, x, 0.0), jnp.where(head0, 0.0, x)], axis=0)

    def each(fn, *cols):
        return [fn(*args) for args in zip(*cols)]

    s, r, v, kk, lw, b, kd = ([u[i] for u in units] for i in range(7))
    hi = each(_bf, lw)
    mid = each(lambda x, h: _bf(x - h.astype(F32)), lw, hi)
    lo = each(lambda x, h, m: _bf(x - h.astype(F32) - m.astype(F32)), lw, hi, mid)
    parts = each(lambda d, h, m, l: jnp.dot(tri[d], jnp.concatenate([h, m, l], axis=1),
                                            preferred_element_type=F32), rev, hi, mid, lo)
    cum = each(lambda x: x[:, 0:LANES] + x[:, LANES:2 * LANES] + x[:, 2 * LANES:3 * LANES], parts)
    total = each(lambda d, x: x[0:1, :] if d else x[c - 1:c, :], rev, cum)
    a_s = each(lambda x, cm, l: stack(-x * jnp.exp(cm - l)), kk, cum, lw)
    r_s = each(lambda x, cm: stack(x * jnp.exp(cm)), r, cum)
    g_inv = each(lambda cm: jnp.exp(-cm), cum)
    g_end = each(lambda tt, cm: jnp.exp(tt - cm), total, cum)
    b_s = each(lambda x, g: stack(x * g), b, g_inv)
    k_s = each(lambda x, g: stack(x * g), kd, g_inv)
    be_s = each(lambda x, g: stack(x * g), b, g_end)
    ke_s = each(lambda x, g: stack(x * g), kd, g_end)
    v_s = each(stack, v)

    a_ab = each(lambda d, x, y: jnp.where(strict[d], _mm_nt(x, y), 0.0), rev, a_s, b_s)
    a_ak = each(lambda d, x, y: jnp.where(strict[d], _mm_nt(x, y), 0.0), rev, a_s, k_s)
    a_rb = each(lambda d, x, y: jnp.where(incl[d], _mm_nt(x, y), 0.0), rev, r_s, b_s)
    a_rk = each(lambda d, x, y: jnp.where(incl[d], _mm_nt(x, y), 0.0), rev, r_s, k_s)

    inv = each(lambda x: eye + x, a_ab)
    power = a_ab
    w = each(_mm, a_ak, v_s)
    for _ in range(int(np.log2(c)) - 1):
        power = each(_mm, power, power)
        inv = each(lambda i, pw: i + _mm(pw, i), inv, power)

    p = each(_mm, inv, a_s)
    q = each(_mm, inv, w)
    qv = each(lambda x, y: jnp.concatenate([x, y], axis=0), q, v_s)
    g_mat = each(lambda x, y, z: x + _mm(y, z), r_s, a_rb, p)
    y0 = each(lambda x, y, z: _mm(jnp.concatenate([x, y], axis=1), z), a_rb, a_rk, qv)
    m_low = each(_mm_tn, be_s, p)
    n_t = each(lambda x, y, z: _mm_tn(x, jnp.concatenate([y, z], axis=0)), qv, be_s, ke_s)

    y = each(lambda g, st, z: _mm_nt(g, st) + z, g_mat, s, y0)
    s_new = each(lambda tt, st, m, n: jnp.exp(tt) * st + _mm_nt(st, m) + n, total, s, m_low, n_t)
    return s_new, each(lambda x: x[0:c, :] + x[c:two, :], y)


def _rwkv_scan_kernel(r_ref, v_ref, kk_ref, lwf_ref, lwb_ref, bf_ref, bb_ref, kdf_ref, kdb_ref,
                      bonus_ref, gate_ref, lnw_ref, lnb_ref, ones_ref, sf0_ref, sb0_ref,
                      o_ref, sf_o, sb_o, yf, yb, *, unroll):
    n_pairs, t = r_ref.shape[0], r_ref.shape[1]
    nc = t // SCAN_CHUNK
    sf_o[...] = sf0_ref[...]
    sb_o[...] = sb0_ref[...]

    def body(ci, carry):
        units, sinks = [], []
        for p in range(n_pairs):
            for reverse, lw_ref, b_ref, kd_ref, s_ref, y_ref in ((False, lwf_ref, bf_ref, kdf_ref, sf_o, yf),
                                                                (True, lwb_ref, bb_ref, kdb_ref, sb_o, yb)):
                cc = (nc - 1 - ci) if reverse else ci
                rows = pl.ds(pl.multiple_of(cc * SCAN_CHUNK, SCAN_CHUNK), SCAN_CHUNK)
                units.append((s_ref[p], r_ref[p, rows, :], v_ref[p, rows, :], kk_ref[p, rows, :],
                              lw_ref[p, rows, :], b_ref[p, rows, :], kd_ref[p, rows, :], reverse))
                sinks.append((s_ref, y_ref, p, rows))
        s_new, y = _scan_blocks(units)
        for (s_ref, y_ref, p, rows), s_val, y_val in zip(sinks, s_new, y):
            s_ref[p] = s_val
            y_ref[p, rows, :] = y_val
        return carry

    lax.fori_loop(0, nc, body, 0, unroll=unroll)

    ones_bd = ones_ref[...]
    for p in range(n_pairs):
        cols = slice(p * LANES, (p + 1) * LANES)
        y = yf[p] + yb[p]
        mean = jnp.dot(y, ones_bd, precision=HIGHEST, preferred_element_type=F32) * (1.0 / RWKV_HEAD)
        dev = y - mean
        var = jnp.dot(dev * dev, ones_bd, precision=HIGHEST, preferred_element_type=F32) * (1.0 / RWKV_HEAD)
        yn = dev * lax.rsqrt(var + LNX_EPS) * lnw_ref[:, cols] + lnb_ref[:, cols]
        o_ref[:, cols] = ((yn + bonus_ref[:, cols]) * gate_ref[:, cols]).astype(o_ref.dtype)


def _rwkv_scan(prep, lnx_w, lnx_b, ones_bd, s0f, s0b, n_seq, t, row_base, pairs_per_step, unroll):
    r, v, kk, lwf, lwb, bf, bb, kdf, kdb, bonus, gate = prep
    base = row_base // t
    pp = pairs_per_step
    width = pp * LANES
    pair_spec = pl.BlockSpec((pp, t, LANES), lambda s, p: (p, base + s, 0))
    row_spec = pl.BlockSpec((t, width), lambda s, p: (base + s, p))
    vec_spec = pl.BlockSpec((1, width), lambda s, p: (0, p))
    s_spec = pl.BlockSpec((None, pp, LANES, LANES), lambda s, p: (s, p, 0, 0))
    s_shape = jax.ShapeDtypeStruct((n_seq, RWKV_PAIRS, LANES, LANES), F32)
    return pl.pallas_call(
        functools.partial(_rwkv_scan_kernel, unroll=unroll),
        out_shape=(jax.ShapeDtypeStruct((n_seq * t, RWKV_WIDTH), BF16), s_shape, s_shape),
        grid=(n_seq, RWKV_PAIRS // pp),
        in_specs=[pair_spec] * 9 + [row_spec, row_spec, vec_spec, vec_spec,
                                    pl.BlockSpec((LANES, LANES), lambda s, p: (0, 0)), s_spec, s_spec],
        out_specs=(pl.BlockSpec((t, width), lambda s, p: (s, p)), s_spec, s_spec),
        scratch_shapes=[pltpu.VMEM((pp, t, LANES), F32), pltpu.VMEM((pp, t, LANES), F32)],
        compiler_params=_params(2),
        name="rwkv_scan",
    )(r, v, kk, lwf, lwb, bf, bb, kdf, kdb, bonus, gate, lnx_w, lnx_b, ones_bd, s0f, s0b)


def _state_to_blockdiag(s):
    n = s.shape[0]
    h = s.reshape(n, RWKV_PAIRS, 2, RWKV_HEAD, RWKV_HEAD)
    z = jnp.zeros_like(h[:, :, 0])
    top = jnp.concatenate([h[:, :, 0], z], axis=-1)
    bot = jnp.concatenate([z, h[:, :, 1]], axis=-1)
    return jnp.concatenate([top, bot], axis=-2)


def _blockdiag_to_state(h):
    n = h.shape[0]
    h0 = h[:, :, :RWKV_HEAD, :RWKV_HEAD]
    h1 = h[:, :, RWKV_HEAD:, RWKV_HEAD:]
    return jnp.stack([h0, h1], axis=2).reshape(n, 2 * RWKV_PAIRS, RWKV_HEAD, RWKV_HEAD)


def _out_proj_kernel(tm_ref, attc_ref, attl_ref, rwc_ref, rwl_ref, sg_ref, w_ref, x_ref, g_ref, o_ref, *, n_ctx_tiles):
    def project(att_ref, rw_ref):
        acc = jnp.dot(att_ref[...], w_ref[0:ATT_WIDTH, :], preferred_element_type=F32)
        acc += jnp.dot(rw_ref[...], w_ref[ATT_WIDTH:ATT_WIDTH + RWKV_WIDTH, :], preferred_element_type=F32)
        acc += jnp.dot(sg_ref[...], w_ref[ATT_WIDTH + RWKV_WIDTH:, :], preferred_element_type=F32)
        o_ref[...] = x_ref[...] + g_ref[0] * acc

    is_ctx = pl.program_id(0) < n_ctx_tiles

    @pl.when(is_ctx)
    def _():
        project(attc_ref, rwc_ref)

    @pl.when(jnp.logical_not(is_ctx))
    def _():
        project(attl_ref, rwl_ref)


def _out_proj(att, rw, sg, w_bf, x, mod48, lay):
    n = x.shape[0]
    tm, tn = 512, 512
    tile_mod = jnp.asarray(_tile_mod_rows(lay, tm))
    nct = lay.n_ctx * lay.t_ctx // tm
    nlt = lay.n_lat * lay.t_lat // tm

    def ctx_rows(i, j, t):
        return (jnp.minimum(i, nct - 1), 0)

    def lat_rows(i, j, t):
        return (jnp.clip(i - nct, 0, nlt - 1), 0)

    return pl.pallas_call(
        functools.partial(_out_proj_kernel, n_ctx_tiles=nct),
        out_shape=jax.ShapeDtypeStruct((n, D_MODEL), F32),
        grid_spec=pltpu.PrefetchScalarGridSpec(
            num_scalar_prefetch=1, grid=(n // tm, D_MODEL // tn),
            in_specs=[pl.BlockSpec((tm, ATT_WIDTH), ctx_rows),
                      pl.BlockSpec((tm, ATT_WIDTH), lat_rows),
                      pl.BlockSpec((tm, RWKV_WIDTH), ctx_rows),
                      pl.BlockSpec((tm, RWKV_WIDTH), lat_rows),
                      pl.BlockSpec((tm, SGU_WIDTH), lambda i, j, t: (i, 0)),
                      pl.BlockSpec((D_MODEL, tn), lambda i, j, t: (0, j)),
                      pl.BlockSpec((tm, tn), lambda i, j, t: (i, j)),
                      pl.BlockSpec((1, 1, tn), lambda i, j, t: (t[i] * 6 + 2, 0, j))],
            out_specs=pl.BlockSpec((tm, tn), lambda i, j, t: (i, j))),
        compiler_params=_params(2),
        name="out_proj",
    )(tile_mod, att[0], att[1], rw[0], rw[1], sg, w_bf, x, mod48)


DISPATCH_ROWS = 4 * MOE_TM


def _dispatch_kernel(src_ref, h_hbm, o_ref, sem):
    base = pl.program_id(0) * DISPATCH_ROWS

    def row_copy(j):
        return pltpu.make_async_copy(h_hbm.at[pl.ds(src_ref[base + j], 1)], o_ref.at[pl.ds(j, 1)], sem)

    def issue(j, carry):
        row_copy(j).start()
        return carry

    def drain(j, carry):
        row_copy(j).wait()
        return carry

    lax.fori_loop(0, DISPATCH_ROWS, issue, 0, unroll=8)
    lax.fori_loop(0, DISPATCH_ROWS, drain, 0, unroll=8)


def _dispatch(h_packed, src):
    r = src.shape[0]
    half = h_packed.shape[1]
    return pl.pallas_call(
        _dispatch_kernel,
        out_shape=jax.ShapeDtypeStruct((r, half), jnp.uint32),
        grid_spec=pltpu.PrefetchScalarGridSpec(
            num_scalar_prefetch=1, grid=(r // DISPATCH_ROWS,),
            in_specs=[pl.BlockSpec(memory_space=pl.ANY)],
            out_specs=pl.BlockSpec((DISPATCH_ROWS, half), lambda i, s: (i, 0)),
            scratch_shapes=[pltpu.SemaphoreType.DMA(())]),
        compiler_params=_params(1),
        name="moe_dispatch",
    )(src, h_packed)


def _unpack_bf16_pair(words):
    lo = lax.bitcast_convert_type(words << 16, F32).astype(BF16)
    hi = lax.bitcast_convert_type(words & jnp.uint32(0xFFFF0000), F32).astype(BF16)
    return lo, hi


def _stream_expert_weights(te_ref, tf_ref, tn_ref, tl_ref, w_hbm, layer, col_starts, width, stage, w_bf, sem, cnt_ref):
    j, i = pl.program_id(0), pl.program_id(1)
    nj = pl.num_programs(0)

    def copies(expert, jj, slot):
        out = []
        for part, start in enumerate(col_starts):
            cols = pl.ds(pl.multiple_of(start + jj * width, width), width)
            out.append(pltpu.make_async_copy(w_hbm.at[layer, expert, :, cols], stage.at[slot, part], sem.at[slot, part]))
        return out

    @pl.when(tf_ref[i] == 1)
    def _():
        @pl.when(jnp.logical_and(j == 0, i == 0))
        def _():
            cnt_ref[0] = 0
            for c in copies(te_ref[0], 0, 0):
                c.start()

        slot = cnt_ref[0] & 1
        for c in copies(te_ref[i], j, slot):
            c.wait()
        next_j = j + tl_ref[i]

        @pl.when(next_j < nj)
        def _():
            for c in copies(tn_ref[i], next_j, 1 - slot):
                c.start()

        for part in range(len(col_starts)):
            w_bf[part] = stage[slot, part].astype(BF16)
        cnt_ref[0] = cnt_ref[0] + 1


def _moe_up_kernel(te_ref, tx_ref, tv_ref, tf_ref, tn_ref, tl_ref, x_ref, w_hbm, bg_ref, bl_ref, o_ref,
                   stage, w_bf, sem, cnt_ref, *, layer):
    i = pl.program_id(1)
    half = D_MODEL // 2
    _stream_expert_weights(te_ref, tf_ref, tn_ref, tl_ref, w_hbm, layer, (0, EXPERT_FF), MOE_TN1,
                           stage, w_bf, sem, cnt_ref)

    @pl.when(tv_ref[i] == 1)
    def _():
        lo, hi = _unpack_bf16_pair(x_ref[...])
        z_glu = (jnp.dot(lo, w_bf[0, 0:half, :], preferred_element_type=F32)
                 + jnp.dot(hi, w_bf[0, half:, :], preferred_element_type=F32) + bg_ref[...])
        z_lin = (jnp.dot(lo, w_bf[1, 0:half, :], preferred_element_type=F32)
                 + jnp.dot(hi, w_bf[1, half:, :], preferred_element_type=F32) + bl_ref[...])
        z_glu = jnp.minimum(z_glu, SWIGLU_LIMIT)
        z_lin = jnp.clip(z_lin, -SWIGLU_LIMIT, SWIGLU_LIMIT)
        o_ref[...] = (z_glu * jax.nn.sigmoid(SWIGLU_ALPHA * z_glu) * (z_lin + 1.0)).astype(o_ref.dtype)

    @pl.when(tv_ref[i] == 0)
    def _():
        o_ref[...] = jnp.zeros_like(o_ref)


def _moe_down_kernel(te_ref, tx_ref, tv_ref, tf_ref, tn_ref, tl_ref, a_ref, w_hbm, b_ref, o_ref,
                     stage, w_bf, sem, cnt_ref, *, layer):
    i = pl.program_id(1)
    _stream_expert_weights(te_ref, tf_ref, tn_ref, tl_ref, w_hbm, layer, (0,), MOE_TN2, stage, w_bf, sem, cnt_ref)

    @pl.when(tv_ref[i] == 1)
    def _():
        y = jnp.dot(a_ref[...], w_bf[0], preferred_element_type=F32) + b_ref[...]
        o_ref[...] = y.astype(o_ref.dtype)

    @pl.when(tv_ref[i] == 0)
    def _():
        o_ref[...] = jnp.zeros_like(o_ref)


def _moe_experts(xs, tiles, w1, b1, w2, b2, layer):
    te, tx, tv, tf, tn, tl = tiles
    r = xs.shape[0]
    n_tiles = r // MOE_TM
    nj1 = EXPERT_FF // MOE_TN1
    hbm = pl.BlockSpec(memory_space=pl.ANY)
    act = pl.pallas_call(
        functools.partial(_moe_up_kernel, layer=layer),
        out_shape=jax.ShapeDtypeStruct((r, EXPERT_FF), BF16),
        grid_spec=pltpu.PrefetchScalarGridSpec(
            num_scalar_prefetch=6, grid=(nj1, n_tiles),
            in_specs=[pl.BlockSpec((MOE_TM, D_MODEL // 2), lambda j, i, te, tx, *_: (tx[i], 0)),
                      hbm,
                      pl.BlockSpec((None, 1, MOE_TN1), lambda j, i, te, *_: (layer * N_EXPERTS + te[i], 0, j)),
                      pl.BlockSpec((None, 1, MOE_TN1), lambda j, i, te, *_: (layer * N_EXPERTS + te[i], 0, nj1 + j))],
            out_specs=pl.BlockSpec((MOE_TM, MOE_TN1), lambda j, i, *_: (i, j)),
            scratch_shapes=[pltpu.VMEM((2, 2, D_MODEL, MOE_TN1), F32), pltpu.VMEM((2, D_MODEL, MOE_TN1), BF16),
                            pltpu.SemaphoreType.DMA((2, 2)), pltpu.SMEM((1,), jnp.int32)]),
        compiler_params=_params(2, MOE_VMEM_LIMIT),
        name="moe_up",
    )(te, tx, tv, tf, tn, tl, xs, w1, b1, b1)
    nj2 = D_MODEL // MOE_TN2
    return pl.pallas_call(
        functools.partial(_moe_down_kernel, layer=layer),
        out_shape=jax.ShapeDtypeStruct((r, D_MODEL), F32),
        grid_spec=pltpu.PrefetchScalarGridSpec(
            num_scalar_prefetch=6, grid=(nj2, n_tiles),
            in_specs=[pl.BlockSpec((MOE_TM, EXPERT_FF), lambda j, i, te, tx, *_: (tx[i], 0)),
                      hbm,
                      pl.BlockSpec((None, 1, MOE_TN2), lambda j, i, te, *_: (layer * N_EXPERTS + te[i], 0, j))],
            out_specs=pl.BlockSpec((MOE_TM, MOE_TN2), lambda j, i, *_: (i, j)),
            scratch_shapes=[pltpu.VMEM((2, 1, EXPERT_FF, MOE_TN2), F32), pltpu.VMEM((1, EXPERT_FF, MOE_TN2), BF16),
                            pltpu.SemaphoreType.DMA((2, 1)), pltpu.SMEM((1,), jnp.int32)]),
        compiler_params=_params(2, MOE_VMEM_LIMIT),
        name="moe_down",
    )(te, tx, tv, tf, tn, tl, act, w2, b2)


def _route(sel, idx4):
    n = sel.shape[0]
    r_max = n * TOP_K + N_EXPERTS * MOE_TM
    n_tiles = r_max // MOE_TM
    sel_i = sel.astype(jnp.int32)
    counts = jnp.sum(sel_i, axis=0)
    tiles_per = (counts + MOE_TM - 1) // MOE_TM
    tile_end = jnp.cumsum(tiles_per)
    offsets = (tile_end - tiles_per) * MOE_TM
    pos_all = offsets[None, :] + jnp.cumsum(sel_i, axis=0) - sel_i
    experts = jnp.arange(N_EXPERTS, dtype=jnp.int32)
    pos4 = jnp.sum(jnp.where(idx4[:, :, None] == experts[None, None, :], pos_all[:, None, :], 0), axis=-1)

    n_used = tile_end[-1]
    tile_ids = jnp.arange(n_tiles, dtype=jnp.int32)
    tx = jnp.minimum(tile_ids, n_used - 1)
    te = jnp.minimum(jnp.sum((tile_end[None, :] <= tx[:, None]).astype(jnp.int32), axis=1), N_EXPERTS - 1)
    tv = (tile_ids < n_used).astype(jnp.int32)
    tf = jnp.concatenate([jnp.ones((1,), jnp.int32), (te[1:] != te[:-1]).astype(jnp.int32)])

    te = te.astype(jnp.int32)
    later_first = (tile_ids[None, :] > tile_ids[:, None]) & (tf[None, :] == 1) & (tv[None, :] == 1)
    next_first = jnp.min(jnp.where(later_first, tile_ids[None, :], n_tiles), axis=1)
    tl = (next_first == n_tiles).astype(jnp.int32)
    tn = jnp.sum(jnp.where(tile_ids[None, :] == jnp.where(tl == 1, 0, next_first)[:, None], te[None, :], 0), axis=1)

    toks = jnp.repeat(jnp.arange(n, dtype=jnp.int32), TOP_K)
    src = jnp.zeros((r_max,), jnp.int32).at[pos4.reshape(-1)].set(toks, unique_indices=True)
    return src, pos4, (te, tx, tv, tf, tn.astype(jnp.int32), tl)


COMBINE_ROWS = TOP_K * ROW_TILE


def _expert_mix(pos_ref, x_ref, gate_ref, g_ref, ys_hbm, buf, sem):
    base = pl.program_id(0) * COMBINE_ROWS

    def row_copy(j):
        return pltpu.make_async_copy(ys_hbm.at[pl.ds(pos_ref[base + j], 1)], buf.at[pl.ds(j, 1)], sem)

    def issue(j, carry):
        row_copy(j).start()
        return carry

    def drain(j, carry):
        row_copy(j).wait()
        return carry

    lax.fori_loop(0, COMBINE_ROWS, issue, 0, unroll=8)
    lax.fori_loop(0, COMBINE_ROWS, drain, 0, unroll=8)
    y = gate_ref[:, 0:1] * buf[0:ROW_TILE, :]
    for k in range(1, TOP_K):
        y += gate_ref[:, k:k + 1] * buf[k * ROW_TILE:(k + 1) * ROW_TILE, :]
    return x_ref[...] + g_ref[0] * y


def _resid_kernel(tm_ref, pos_ref, x_ref, gate_ref, g_ref, ys_hbm, o_ref, buf, sem):
    o_ref[...] = _expert_mix(pos_ref, x_ref, gate_ref, g_ref, ys_hbm, buf, sem)


def _resid_norm_kernel(tm_ref, pos_ref, x_ref, gate_ref, g_ref, fn_ref, ys_hbm, o_ref, buf, sem):
    x = _expert_mix(pos_ref, x_ref, gate_ref, g_ref, ys_hbm, buf, sem)
    o_ref[...] = x * lax.rsqrt(jnp.mean(x * x, axis=-1, keepdims=True) + NORM_EPS) * fn_ref[...]


def _gated_residual(x, out_sorted, pos4, gates, mod48, lay, final_norm=None):
    n = x.shape[0]
    nt = n // ROW_TILE
    tile_mod = jnp.asarray(_tile_mod_rows(lay, ROW_TILE))
    pos_flat = pos4.reshape(nt, ROW_TILE, TOP_K).transpose(0, 2, 1).reshape(-1)
    tile = pl.BlockSpec((ROW_TILE, D_MODEL), lambda i, t, q: (i, 0))
    in_specs = [tile, pl.BlockSpec((ROW_TILE, LANES), lambda i, t, q: (i, 0)),
                pl.BlockSpec((1, 1, D_MODEL), lambda i, t, q: (t[i] * 6 + 5, 0, 0))]
    args = [tile_mod, pos_flat, x, gates, mod48]
    kern = _resid_kernel
    if final_norm is not None:
        in_specs.append(pl.BlockSpec((1, D_MODEL), lambda i, t, q: (0, 0)))
        args.append(final_norm)
        kern = _resid_norm_kernel
    in_specs.append(pl.BlockSpec(memory_space=pl.ANY))
    args.append(out_sorted)
    return pl.pallas_call(
        kern,
        out_shape=jax.ShapeDtypeStruct((n, D_MODEL), F32),
        grid_spec=pltpu.PrefetchScalarGridSpec(
            num_scalar_prefetch=2, grid=(nt,), in_specs=in_specs, out_specs=tile,
            scratch_shapes=[pltpu.VMEM((COMBINE_ROWS, D_MODEL), F32), pltpu.SemaphoreType.DMA(())]),
        compiler_params=_params(1),
        name="gated_residual",
    )(*args)


def _block_ones():
    idx = np.arange(LANES) // RWKV_HEAD
    return jnp.asarray((idx[:, None] == idx[None, :]).astype(np.float32))


def _layer_weights(l, w_in, w_out, rwkv_mu, w0, w_up, a0, a_up, g_up, k_k, k_a, r_k, sgu_w, sgu_b, w_router, b_router):
    w_sgu = w_in[l][None, :, SGU_IN_START:].astype(BF16)
    mu = rwkv_mu[l]
    zeros_lora = jnp.zeros((DECAY_LORA, RWKV_WIDTH), F32)

    def blockdiag(up):
        return jnp.concatenate([jnp.concatenate([up[0], zeros_lora], axis=1),
                                jnp.concatenate([zeros_lora, up[1]], axis=1)], axis=0).astype(BF16)

    rp = {
        "mu_r": mu[None, 0:RWKV_WIDTH], "mu_k": mu[None, RWKV_WIDTH:2 * RWKV_WIDTH],
        "mu_v": mu[None, 2 * RWKV_WIDTH:3 * RWKV_WIDTH],
        "mu_z": jnp.pad(mu[3 * RWKV_WIDTH:], (0, Z_WIDTH - (RWKV_IN - 3 * RWKV_WIDTH)))[None],
        "k_k": k_k[l][None], "k_a": k_a[l][None], "r_k": r_k[l].reshape(1, RWKV_WIDTH),
        "w0": w0[l], "a0": a0[l], "w_up": blockdiag(w_up[l]), "a_up": blockdiag(a_up[l]),
        "g_up": jnp.pad(g_up[l], ((0, 2 * LANES - GATE_LORA), (0, 0))).astype(BF16),
        "ones_bd": _block_ones(),
    }
    wr = jnp.pad(w_router[l], ((0, 0), (0, LANES - N_EXPERTS)))
    br = jnp.pad(b_router[l], (0, LANES - N_EXPERTS))[None]
    return {"w_sgu": w_sgu, "w_out": w_out[l].astype(BF16), "rp": rp,
            "sgu_w": sgu_w[l].astype(BF16), "sgu_bt": sgu_b[l].T, "router": (wr, br)}


def _layer(x, l, lay, mod48, lw, p, cache_k, cache_v, h0_lat, rope):
    n_ctx_rows = lay.n_ctx * lay.t_ctx
    h = _norm_mod(x, p["norm_mix"][l][None], mod48, 0, lay)
    tm = 1024 if x.shape[0] % 1024 == 0 else 512
    proj = _matmul(h, p["w_in"], l, P_WIDTH, tm, 512)
    proj_sgu = _matmul(h, lw["w_sgu"], 0, 2 * SGU_WIDTH, tm, 512)

    qn, kn = p["q_norm"][l][None], p["k_norm"][l][None]
    att_ctx, k_ctx = _attention_ctx(proj, qn, kn, lay)
    att_lat = _attention_lat(proj, cache_k, cache_v, rope[0], rope[1], qn, kn, l, lay)
    v_ctx = proj[:n_ctx_rows, COL_V:COL_V + KV_WIDTH]

    prep = _rwkv_prep(proj, lw["rp"], lay)
    lnw, lnb, ones_bd = p["lnx_w"][l][None], p["lnx_b"][l][None], lw["rp"]["ones_bd"]
    zeros_h = jnp.zeros((lay.n_ctx, RWKV_PAIRS, LANES, LANES), F32)
    rw_ctx, hf_ctx, hb_ctx = _rwkv_scan(prep, lnw, lnb, ones_bd, zeros_h, zeros_h, lay.n_ctx, lay.t_ctx, 0,
                                        pairs_per_step=SCAN_PAIRS_CTX, unroll=1)
    rw_lat, _, _ = _rwkv_scan(prep, lnw, lnb, ones_bd, h0_lat[0], h0_lat[1], lay.n_lat, lay.t_lat, n_ctx_rows,
                              pairs_per_step=SCAN_PAIRS_LAT, unroll=1)

    sg = _sgu(proj_sgu, lw["sgu_w"], lw["sgu_bt"], p["sgu_norm"][l][None])
    x = _out_proj((att_ctx, att_lat), (rw_ctx, rw_lat), sg, lw["w_out"], x, mod48, lay)

    h2, gates, idx, sel = _norm_mod(x, p["norm_ffn"][l][None], mod48, 3, lay, router=lw["router"])
    src, pos4, tiles = _route(sel[:, :N_EXPERTS], idx[:, :TOP_K])
    xs = _dispatch(h2, src)
    out_sorted = _moe_experts(xs, tiles, p["w_exp1"], p["b_exp1r"], p["w_exp2"], p["b_exp2r"], l)
    final = p["final_norm"][None] if l == DEPTH - 1 else None
    x = _gated_residual(x, out_sorted, pos4, gates, mod48, lay, final)
    return x, k_ctx, v_ctx, hf_ctx, hb_ctx


def _forward(lay, x_prompt, x_sample, cache_k, cache_v, state_wkv_fwd, state_wkv_bwd, c, c_ctx, p):
    n_ctx_rows = lay.n_ctx * lay.t_ctx
    x = jnp.concatenate([x_prompt.reshape(n_ctx_rows, D_MODEL), x_sample.reshape(lay.n_lat * lay.t_lat, D_MODEL)])
    cond8 = jnp.zeros((8, D_MODEL), F32).at[0].set(c_ctx).at[1:1 + lay.n_lat].set(c)
    rope = _rope_tables(lay.t_lat)
    ck = cache_k.reshape(lay.n_lat, DEPTH, lay.past, KV_WIDTH)
    cv = cache_v.reshape(lay.n_lat, DEPTH, lay.past, KV_WIDTH)
    p = dict(p)
    p["b_exp1r"] = p["b_exp1"].reshape(DEPTH * N_EXPERTS, 1, 2 * EXPERT_FF)
    p["b_exp2r"] = p["b_exp2"].reshape(DEPTH * N_EXPERTS, 1, D_MODEL)
    ks, vs, sfs, sbs = [], [], [], []
    for l in range(DEPTH):
        lw = _layer_weights(l, p["w_in"], p["w_out"], p["rwkv_mu"], p["w0"], p["w_up"], p["a0"], p["a_up"],
                            p["g_up"], p["k_k"], p["k_a"], p["r_k"], p["sgu_w"], p["sgu_b"],
                            p["w_router"], p["b_router"])
        mod = _modulation(cond8, p["w_mod"], p["b_mod"], l)
        mod48 = mod.reshape(8 * 6, 1, D_MODEL)
        h0_lat = (_state_to_blockdiag(state_wkv_fwd[:, l]), _state_to_blockdiag(state_wkv_bwd[:, l]))
        x, k_l, v_l, hf, hb = _layer(x, l, lay, mod48, lw, p, ck, cv, h0_lat, rope)
        ks.append(k_l.reshape(lay.n_ctx, lay.t_ctx, ATT_KV_HEADS, HEAD_DIM))
        vs.append(v_l.reshape(lay.n_ctx, lay.t_ctx, ATT_KV_HEADS, HEAD_DIM))
        sfs.append(_blockdiag_to_state(hf))
        sbs.append(_blockdiag_to_state(hb))
    y_prompt = x[:n_ctx_rows].reshape(x_prompt.shape)
    y_sample = x[n_ctx_rows:].reshape(x_sample.shape)
    return (y_prompt, y_sample, jnp.stack(ks, axis=1), jnp.stack(vs, axis=1),
            jnp.stack(sfs, axis=1), jnp.stack(sbs, axis=1))


def kernel(x_prompt, x_sample, cache_k, cache_v, state_wkv_fwd, state_wkv_bwd, c, c_ctx, norm_mix, norm_ffn, w_mod, b_mod, w_in, w_out, q_norm, k_norm, rwkv_mu, w0, w_up, a0, a_up, g_up, k_k, k_a, r_k, lnx_w, lnx_b, sgu_norm, sgu_w, sgu_b, w_router, b_router, w_exp1, b_exp1, w_exp2, b_exp2, final_norm):
    lay = Layout(n_ctx=x_prompt.shape[0], t_ctx=x_prompt.shape[1], n_lat=x_sample.shape[0],
                 t_lat=x_sample.shape[1], past=cache_k.shape[2])
    p = dict(norm_mix=norm_mix, norm_ffn=norm_ffn, w_mod=w_mod, b_mod=b_mod, w_in=w_in, w_out=w_out,
             q_norm=q_norm, k_norm=k_norm, rwkv_mu=rwkv_mu, w0=w0, w_up=w_up, a0=a0, a_up=a_up, g_up=g_up,
             k_k=k_k, k_a=k_a, r_k=r_k, lnx_w=lnx_w, lnx_b=lnx_b, sgu_norm=sgu_norm, sgu_w=sgu_w, sgu_b=sgu_b,
             w_router=w_router, b_router=b_router, w_exp1=w_exp1, b_exp1=b_exp1, w_exp2=w_exp2, b_exp2=b_exp2,
             final_norm=final_norm)
    return _forward(lay, x_prompt, x_sample, cache_k, cache_v, state_wkv_fwd, state_wkv_bwd, c, c_ctx, p)
```

```python
import collections
import functools

import numpy as np
import jax
import jax.numpy as jnp
from jax import lax
from jax.experimental import pallas as pl
from jax.experimental.pallas import tpu as pltpu

F32 = jnp.float32
BF16 = jnp.bfloat16
HIGHEST = lax.Precision.HIGHEST

D_MODEL = 4096
DEPTH = 2
GRID_W = 64
HEAD_DIM = 128
ATT_WIDTH = 2048
ATT_KV_HEADS = 4
ATT_GROUP = 4
KV_WIDTH = 512
ROPE_THETA = 10000.0
ROPE_FREQS = 32
RWKV_WIDTH = 1024
RWKV_HEAD = 64
RWKV_PAIRS = 8
DECAY_LORA = 64
ICLR_LORA = 64
GATE_LORA = 160
RWKV_IN = 3 * RWKV_WIDTH + 2 * DECAY_LORA + 2 * ICLR_LORA + GATE_LORA
SGU_WIDTH = 1024
SGU_GROUPS = 8
CHUNK = 128
N_EXPERTS = 32
TOP_K = 4
EXPERT_FF = 2048
SWIGLU_LIMIT = 7.0
SWIGLU_ALPHA = 1.702
NORM_EPS = 1e-6
LNX_EPS = 64e-5
L2_EPS = 1e-12

LANES = 128
ROW_TILE = 256
SCAN_CHUNK = 64
SCAN_PAIRS_CTX = 4
SCAN_PAIRS_LAT = 2
VMEM_LIMIT = 48 * 1024 * 1024

COL_Q, COL_K, COL_V, COL_R, COL_RK, COL_RV, COL_Z = 0, 2048, 2560, 3072, 4096, 5120, 6144
Z_WIDTH = 512
P_WIDTH = COL_Z + Z_WIDTH
SGU_IN_START = ATT_WIDTH + 2 * KV_WIDTH + RWKV_IN
COL_U, COL_SV = 0, 1024

MOE_TM = 256
MOE_TN1 = 512
MOE_TN2 = 2048
MOE_VMEM_LIMIT = 56 * 1024 * 1024

Layout = collections.namedtuple("Layout", "n_ctx t_ctx n_lat t_lat past")


def _params(n_axes, vmem_limit=VMEM_LIMIT):
    return pltpu.CompilerParams(dimension_semantics=("arbitrary",) * n_axes, vmem_limit_bytes=vmem_limit)


def _rows(lay):
    return lay.n_ctx * lay.t_ctx + lay.n_lat * lay.t_lat


def _tile_mod_rows(lay, tile):
    n_ctx_rows = lay.n_ctx * lay.t_ctx
    starts = np.arange(0, _rows(lay), tile)
    return np.where(starts < n_ctx_rows, 0, 1 + (starts - n_ctx_rows) // lay.t_lat).astype(np.int32)


def _mod_kernel(c_ref, w_ref, b_ref, o_ref):
    c = c_ref[...]
    a = (c * jax.nn.sigmoid(c)).astype(BF16)
    o_ref[...] = jnp.dot(a, w_ref[...].astype(BF16), preferred_element_type=F32) + b_ref[...]


def _modulation(cond8, w_mod, b_mod, layer):
    tn = 512
    n = w_mod.shape[-1]
    return pl.pallas_call(
        _mod_kernel,
        out_shape=jax.ShapeDtypeStruct((8, n), F32),
        grid=(n // tn,),
        in_specs=[
            pl.BlockSpec((8, D_MODEL), lambda j: (0, 0)),
            pl.BlockSpec((None, D_MODEL, tn), lambda j: (layer, 0, j)),
            pl.BlockSpec((None, 1, tn), lambda j: (layer, 0, j)),
        ],
        out_specs=pl.BlockSpec((8, tn), lambda j: (0, j)),
        compiler_params=_params(1),
        name="modulation",
    )(cond8, w_mod, b_mod.reshape(DEPTH, 1, n))


def _modulated_norm(x_ref, g_ref, sh_ref, sc_ref):
    x = x_ref[...]
    y = x * lax.rsqrt(jnp.mean(x * x, axis=-1, keepdims=True) + NORM_EPS) * g_ref[...]
    return y * (1.0 + sc_ref[0]) + sh_ref[0]


def _norm_mod_kernel(tm_ref, x_ref, g_ref, sh_ref, sc_ref, o_ref):
    o_ref[...] = _modulated_norm(x_ref, g_ref, sh_ref, sc_ref).astype(o_ref.dtype)


def _norm_router_kernel(tm_ref, x_ref, g_ref, sh_ref, sc_ref, wr_ref, br_ref, o_ref, gate_ref, idx_ref, sel_ref):
    h = _modulated_norm(x_ref, g_ref, sh_ref, sc_ref)
    bits = lax.bitcast_convert_type(h.astype(BF16).astype(F32), jnp.uint32)
    o_ref[...] = (bits[:, :D_MODEL // 2] >> 16) | bits[:, D_MODEL // 2:]
    logits = jnp.dot(h, wr_ref[...], precision=HIGHEST, preferred_element_type=F32) + br_ref[...]
    lane = lax.broadcasted_iota(jnp.int32, logits.shape, 1)
    neg = jnp.float32(-jnp.inf)
    work = jnp.where(lane < N_EXPERTS, logits, neg)
    tops, firsts = [], []
    sel = jnp.zeros_like(logits)
    for _ in range(TOP_K):
        m = jnp.max(work, axis=-1, keepdims=True)
        first = jnp.min(jnp.where(work == m, lane, LANES), axis=-1, keepdims=True)
        hot = lane == first
        tops.append(m)
        firsts.append(first)
        sel = jnp.where(hot, 1.0, sel)
        work = jnp.where(hot, neg, work)
    exps = [jnp.exp(t - tops[0]) for t in tops]
    denom = exps[0] + exps[1] + exps[2] + exps[3]
    gates = jnp.zeros_like(logits)
    idx = jnp.zeros(logits.shape, jnp.int32)
    for k in range(TOP_K):
        gates = jnp.where(lane == k, exps[k] / denom, gates)
        idx = jnp.where(lane == k, firsts[k], idx)
    gate_ref[...] = gates
    idx_ref[...] = idx
    sel_ref[...] = sel


def _norm_mod(x, gain, mod48, which_shift, lay, router=None):
    n = x.shape[0]
    tile_mod = jnp.asarray(_tile_mod_rows(lay, ROW_TILE))
    in_specs = [
        pl.BlockSpec((ROW_TILE, D_MODEL), lambda i, tm: (i, 0)),
        pl.BlockSpec((1, D_MODEL), lambda i, tm: (0, 0)),
        pl.BlockSpec((1, 1, D_MODEL), lambda i, tm: (tm[i] * 6 + which_shift, 0, 0)),
        pl.BlockSpec((1, 1, D_MODEL), lambda i, tm: (tm[i] * 6 + which_shift + 1, 0, 0)),
    ]
    h_spec = pl.BlockSpec((ROW_TILE, D_MODEL), lambda i, tm: (i, 0))
    if router is None:
        return pl.pallas_call(
            _norm_mod_kernel,
            out_shape=jax.ShapeDtypeStruct((n, D_MODEL), BF16),
            grid_spec=pltpu.PrefetchScalarGridSpec(
                num_scalar_prefetch=1, grid=(n // ROW_TILE,), in_specs=in_specs, out_specs=h_spec),
            compiler_params=_params(1),
            name="norm_mod",
        )(tile_mod, x, gain, mod48, mod48)
    wr, br = router
    lane_spec = pl.BlockSpec((ROW_TILE, LANES), lambda i, tm: (i, 0))
    return pl.pallas_call(
        _norm_router_kernel,
        out_shape=(jax.ShapeDtypeStruct((n, D_MODEL // 2), jnp.uint32),
                   jax.ShapeDtypeStruct((n, LANES), F32),
                   jax.ShapeDtypeStruct((n, LANES), jnp.int32),
                   jax.ShapeDtypeStruct((n, LANES), F32)),
        grid_spec=pltpu.PrefetchScalarGridSpec(
            num_scalar_prefetch=1, grid=(n // ROW_TILE,),
            in_specs=in_specs + [pl.BlockSpec((D_MODEL, LANES), lambda i, tm: (0, 0)),
                                 pl.BlockSpec((1, LANES), lambda i, tm: (0, 0))],
            out_specs=(pl.BlockSpec((ROW_TILE, D_MODEL // 2), lambda i, tm: (i, 0)),
                       lane_spec, lane_spec, lane_spec)),
        compiler_params=_params(1),
        name="norm_router",
    )(tile_mod, x, gain, mod48, mod48, wr, br)


def _mm_kernel(a_ref, w_ref, o_ref):
    o_ref[...] = jnp.dot(a_ref[...], w_ref[...].astype(BF16), preferred_element_type=F32)


def _matmul(a, w, layer, n, tm, tn):
    m, k = a.shape
    return pl.pallas_call(
        _mm_kernel,
        out_shape=jax.ShapeDtypeStruct((m, n), F32),
        grid=(m // tm, n // tn),
        in_specs=[pl.BlockSpec((tm, k), lambda i, j: (i, 0)),
                  pl.BlockSpec((None, k, tn), lambda i, j: (layer, 0, j))],
        out_specs=pl.BlockSpec((tm, tn), lambda i, j: (i, j)),
        compiler_params=_params(2),
        name="in_proj",
    )(a, w)


def _head_norm(x, g):
    return x * lax.rsqrt(jnp.mean(x * x, axis=-1, keepdims=True) + NORM_EPS) * g


def _rope(x, cos, sin_signed):
    lane = lax.broadcasted_iota(jnp.int32, x.shape, 1)
    partner = jnp.where((lane & 32) == 0, pltpu.roll(x, LANES - 32, 1), pltpu.roll(x, 32, 1))
    return x * cos + partner * sin_signed


def _softmax_pv(q_bf, k_bf, v_bf):
    s = lax.dot_general(q_bf, k_bf, (((1,), (1,)), ((), ())), preferred_element_type=F32) * (HEAD_DIM ** -0.5)
    e = jnp.exp(s - jnp.max(s, axis=-1, keepdims=True))
    o = jnp.dot(e.astype(BF16), v_bf, preferred_element_type=F32)
    return o / jnp.sum(e, axis=-1, keepdims=True)


def _attn_ctx_kernel(q_ref, k_ref, v_ref, qn_ref, kn_ref, o_ref, ko_ref):
    t = q_ref.shape[0]
    kn = _head_norm(k_ref[...], kn_ref[...])
    ko_ref[...] = kn
    qs = [_head_norm(q_ref[:, h * HEAD_DIM:(h + 1) * HEAD_DIM], qn_ref[...]) for h in range(ATT_GROUP)]
    q = jnp.concatenate(qs, axis=0).astype(BF16)
    o = _softmax_pv(q, kn.astype(BF16), v_ref[...].astype(BF16))
    for h in range(ATT_GROUP):
        o_ref[:, h * HEAD_DIM:(h + 1) * HEAD_DIM] = o[h * t:(h + 1) * t].astype(o_ref.dtype)


def _attention_ctx(proj, qn, kn, lay):
    t = lay.t_ctx
    rows = lay.n_ctx * t
    kcol, vcol = COL_K // HEAD_DIM, COL_V // HEAD_DIM
    return pl.pallas_call(
        _attn_ctx_kernel,
        out_shape=(jax.ShapeDtypeStruct((rows, ATT_WIDTH), BF16), jax.ShapeDtypeStruct((rows, KV_WIDTH), F32)),
        grid=(lay.n_ctx, ATT_KV_HEADS),
        in_specs=[pl.BlockSpec((t, ATT_GROUP * HEAD_DIM), lambda b, g: (b, g)),
                  pl.BlockSpec((t, HEAD_DIM), lambda b, g: (b, kcol + g)),
                  pl.BlockSpec((t, HEAD_DIM), lambda b, g: (b, vcol + g)),
                  pl.BlockSpec((1, HEAD_DIM), lambda b, g: (0, 0)),
                  pl.BlockSpec((1, HEAD_DIM), lambda b, g: (0, 0))],
        out_specs=(pl.BlockSpec((t, ATT_GROUP * HEAD_DIM), lambda b, g: (b, g)),
                   pl.BlockSpec((t, HEAD_DIM), lambda b, g: (b, g))),
        compiler_params=_params(2),
        name="attention_ctx",
    )(proj, proj, proj, qn, kn)


def _attn_lat_kernel(q_ref, k_ref, v_ref, ck_ref, cv_ref, cosq_ref, sinq_ref, cosk_ref, sink_ref,
                     qn_ref, kn_ref, o_ref, kbuf, vbuf):
    past = ck_ref.shape[0]
    tq = q_ref.shape[0]

    @pl.when(pl.program_id(2) == 0)
    def _():
        kr = _rope(_head_norm(k_ref[...], kn_ref[...]), cosk_ref[...], sink_ref[...])
        kbuf[0:past, :] = ck_ref[...].astype(BF16)
        kbuf[past:, :] = kr.astype(BF16)
        vbuf[0:past, :] = cv_ref[...].astype(BF16)
        vbuf[past:, :] = v_ref[...].astype(BF16)

    cos, sin = cosq_ref[...], sinq_ref[...]
    qs = [_rope(_head_norm(q_ref[:, h * HEAD_DIM:(h + 1) * HEAD_DIM], qn_ref[...]), cos, sin)
          for h in range(ATT_GROUP)]
    q = jnp.concatenate(qs, axis=0).astype(BF16)
    o = _softmax_pv(q, kbuf[...], vbuf[...])
    for h in range(ATT_GROUP):
        o_ref[:, h * HEAD_DIM:(h + 1) * HEAD_DIM] = o[h * tq:(h + 1) * tq].astype(o_ref.dtype)


def _attention_lat(proj, cache_k, cache_v, cos, sin, qn, kn, layer, lay):
    t, past = lay.t_lat, lay.past
    tq = ROW_TILE
    nq = t // tq
    base_q = lay.n_ctx * lay.t_ctx // tq
    base_t = lay.n_ctx * lay.t_ctx // t
    kcol, vcol = COL_K // HEAD_DIM, COL_V // HEAD_DIM
    cache_spec = pl.BlockSpec((None, None, past, HEAD_DIM), lambda b, g, i: (b, layer, 0, g))
    return pl.pallas_call(
        _attn_lat_kernel,
        out_shape=jax.ShapeDtypeStruct((lay.n_lat * t, ATT_WIDTH), BF16),
        grid=(lay.n_lat, ATT_KV_HEADS, nq),
        in_specs=[pl.BlockSpec((tq, ATT_GROUP * HEAD_DIM), lambda b, g, i: (base_q + b * nq + i, g)),
                  pl.BlockSpec((t, HEAD_DIM), lambda b, g, i: (base_t + b, kcol + g)),
                  pl.BlockSpec((t, HEAD_DIM), lambda b, g, i: (base_t + b, vcol + g)),
                  cache_spec, cache_spec,
                  pl.BlockSpec((tq, HEAD_DIM), lambda b, g, i: (i, 0)),
                  pl.BlockSpec((tq, HEAD_DIM), lambda b, g, i: (i, 0)),
                  pl.BlockSpec((t, HEAD_DIM), lambda b, g, i: (0, 0)),
                  pl.BlockSpec((t, HEAD_DIM), lambda b, g, i: (0, 0)),
                  pl.BlockSpec((1, HEAD_DIM), lambda b, g, i: (0, 0)),
                  pl.BlockSpec((1, HEAD_DIM), lambda b, g, i: (0, 0))],
        out_specs=pl.BlockSpec((tq, ATT_GROUP * HEAD_DIM), lambda b, g, i: (b * nq + i, g)),
        scratch_shapes=[pltpu.VMEM((past + t, HEAD_DIM), BF16), pltpu.VMEM((past + t, HEAD_DIM), BF16)],
        compiler_params=_params(3),
        name="attention_lat",
    )(proj, proj, proj, cache_k, cache_v, cos, sin, cos, sin, qn, kn)


def _rope_tables(t):
    pos = np.arange(t)
    inv = ROPE_THETA ** (-np.arange(ROPE_FREQS, dtype=np.float32) / ROPE_FREQS)
    ang_r = (pos // GRID_W).astype(np.float32)[:, None] * inv
    ang_c = (pos % GRID_W).astype(np.float32)[:, None] * inv
    cr, sr, cc, sc = np.cos(ang_r), np.sin(ang_r), np.cos(ang_c), np.sin(ang_c)
    cos = np.concatenate([cr, cr, cc, cc], axis=1)
    sin = np.concatenate([-sr, sr, -sc, sc], axis=1)
    return jnp.asarray(cos, F32), jnp.asarray(sin, F32)


def _sgu_kernel(u_ref, sv_ref, w_ref, bt_ref, g_ref, o_ref):
    for g in range(SGU_GROUPS):
        cols = slice(g * LANES, (g + 1) * LANES)
        sv = sv_ref[:, cols]
        svn = sv * lax.rsqrt(jnp.mean(sv * sv, axis=-1, keepdims=True) + NORM_EPS) * g_ref[:, cols]
        z = jnp.dot(w_ref[g], svn.astype(BF16), preferred_element_type=F32) + bt_ref[:, g:g + 1]
        o_ref[:, cols] = (u_ref[:, cols] * z).astype(o_ref.dtype)


def _sgu(proj, w_bf, b_t, gain):
    n = proj.shape[0]
    ucol, svcol = COL_U // SGU_WIDTH, COL_SV // SGU_WIDTH
    return pl.pallas_call(
        _sgu_kernel,
        out_shape=jax.ShapeDtypeStruct((n, SGU_WIDTH), BF16),
        grid=(n // CHUNK,),
        in_specs=[pl.BlockSpec((CHUNK, SGU_WIDTH), lambda i: (i, ucol)),
                  pl.BlockSpec((CHUNK, SGU_WIDTH), lambda i: (i, svcol)),
                  pl.BlockSpec((SGU_GROUPS, CHUNK, CHUNK), lambda i: (0, 0, 0)),
                  pl.BlockSpec((CHUNK, SGU_GROUPS), lambda i: (0, 0)),
                  pl.BlockSpec((1, SGU_WIDTH), lambda i: (0, 0))],
        out_specs=pl.BlockSpec((CHUNK, SGU_WIDTH), lambda i: (i, 0)),
        compiler_params=_params(1),
        name="sgu",
    )(proj, proj, w_bf, b_t, gain)


def _pair_sum(x, ones_bd):
    blocks = [jnp.dot(x[:, p * LANES:(p + 1) * LANES], ones_bd, precision=HIGHEST, preferred_element_type=F32)
              for p in range(x.shape[1] // LANES)]
    return jnp.concatenate(blocks, axis=1)


def _rwkv_prep_kernel(first_ref, last_ref,
                      r_ref, k_ref, v_ref, z_ref, rp_ref, kp_ref, vp_ref, zp_ref, rn_ref, kn_ref, vn_ref, zn_ref,
                      mur_ref, muk_ref, muv_ref, muz_ref, kk_ref, ka_ref, rk_ref, w0_ref, a0_ref,
                      wup_ref, aup_ref, gup_ref, ones_ref,
                      r_o, v_o, kk_o, lwf_o, lwb_o, bf_o, bb_o, kdf_o, kdb_o, bonus_o, gate_o):
    i = pl.program_id(0)
    has_prev = first_ref[i] == 0
    has_next = last_ref[i] == 0
    row = lax.broadcasted_iota(jnp.int32, (ROW_TILE, 1), 0)

    def mix(x_ref, p_ref, n_ref, mu_ref):
        x = x_ref[...]
        prev = jnp.where(row == 0, jnp.where(has_prev, p_ref[7:8, :], 0.0), pltpu.roll(x, 1, 0))
        nxt = jnp.where(row == ROW_TILE - 1, jnp.where(has_next, n_ref[0:1, :], 0.0),
                        pltpu.roll(x, ROW_TILE - 1, 0))
        return x + (0.5 * (prev + nxt) - x) * mu_ref[...]

    r = mix(r_ref, rp_ref, rn_ref, mur_ref)
    k = mix(k_ref, kp_ref, kn_ref, muk_ref)
    v = mix(v_ref, vp_ref, vn_ref, muv_ref)
    z = mix(z_ref, zp_ref, zn_ref, muz_ref)
    ones_bd = ones_ref[...]

    kk = k * kk_ref[...]
    kk = kk / jnp.maximum(jnp.sqrt(_pair_sum(kk * kk, ones_bd)), L2_EPS)

    lora_w = jnp.dot(jnp.tanh(z[:, 0:LANES]).astype(BF16), wup_ref[...], preferred_element_type=F32)
    lora_a = jnp.dot(z[:, LANES:2 * LANES].astype(BF16), aup_ref[...], preferred_element_type=F32)
    gate_o[...] = jnp.dot(jax.nn.sigmoid(z[:, 2 * LANES:4 * LANES]).astype(BF16), gup_ref[...],
                          preferred_element_type=F32)

    iclr_sum = jnp.zeros_like(k)
    for d, (lw_o, b_o, kd_o) in enumerate(((lwf_o, bf_o, kdf_o), (lwb_o, bb_o, kdb_o))):
        cols = slice(d * RWKV_WIDTH, (d + 1) * RWKV_WIDTH)
        x = -(w0_ref[d:d + 1, :] + lora_w[:, cols])
        softplus = jnp.maximum(x, 0.0) + jnp.log(1.0 + jnp.exp(-jnp.abs(x)))
        log_decay = -jnp.exp(-softplus - 0.5)
        iclr = jax.nn.sigmoid(a0_ref[d:d + 1, :] + lora_a[:, cols])
        kd = k * (1.0 + (iclr - 1.0) * ka_ref[...])
        b = kk * iclr
        iclr_sum = iclr_sum + iclr
        for p in range(RWKV_PAIRS):
            pc = slice(p * LANES, (p + 1) * LANES)
            lw_o[p] = log_decay[:, pc]
            b_o[p] = b[:, pc]
            kd_o[p] = kd[:, pc]
    for p in range(RWKV_PAIRS):
        pc = slice(p * LANES, (p + 1) * LANES)
        r_o[p] = r[:, pc]
        v_o[p] = v[:, pc]
        kk_o[p] = kk[:, pc]
    bonus_o[...] = _pair_sum(r * k * rk_ref[...] * (2.0 + (iclr_sum - 2.0) * ka_ref[...]), ones_bd) * v


def _seq_edge_flags(lay):
    starts = np.arange(0, _rows(lay), ROW_TILE)
    n_ctx_rows = lay.n_ctx * lay.t_ctx
    in_seq = np.where(starts < n_ctx_rows, starts % lay.t_ctx, (starts - n_ctx_rows) % lay.t_lat)
    seq_len = np.where(starts < n_ctx_rows, lay.t_ctx, lay.t_lat)
    first = (in_seq == 0).astype(np.int32)
    last = (in_seq + ROW_TILE == seq_len).astype(np.int32)
    return jnp.asarray(first), jnp.asarray(last)


def _rwkv_prep(proj, rp, lay):
    n = proj.shape[0]
    n_tiles = n // ROW_TILE
    first, last = _seq_edge_flags(lay)
    sub = ROW_TILE // 8
    wcols = {"r": (RWKV_WIDTH, COL_R // RWKV_WIDTH), "k": (RWKV_WIDTH, COL_RK // RWKV_WIDTH),
             "v": (RWKV_WIDTH, COL_RV // RWKV_WIDTH), "z": (Z_WIDTH, COL_Z // Z_WIDTH)}
    main = [pl.BlockSpec((ROW_TILE, w), functools.partial(lambda i, f, l, c: (i, c), c=c))
            for w, c in wcols.values()]
    prev = [pl.BlockSpec((8, w), functools.partial(lambda i, f, l, c: (jnp.maximum(i * sub - 1, 0), c), c=c))
            for w, c in wcols.values()]
    nxt = [pl.BlockSpec((8, w), functools.partial(
        lambda i, f, l, c: (jnp.minimum((i + 1) * sub, n_tiles * sub - 1), c), c=c)) for w, c in wcols.values()]

    def const(shape):
        return pl.BlockSpec(shape, lambda i, f, l: (0,) * len(shape))

    consts = [const((1, RWKV_WIDTH))] * 3 + [const((1, Z_WIDTH))] + [const((1, RWKV_WIDTH))] * 3 + \
        [const((2, RWKV_WIDTH))] * 2 + [const((LANES, 2 * RWKV_WIDTH))] * 2 + \
        [const((2 * LANES, RWKV_WIDTH)), const((LANES, LANES))]
    pair_out = jax.ShapeDtypeStruct((RWKV_PAIRS, n, LANES), F32)
    pair_spec = pl.BlockSpec((RWKV_PAIRS, ROW_TILE, LANES), lambda i, f, l: (0, i, 0))
    row_out = jax.ShapeDtypeStruct((n, RWKV_WIDTH), F32)
    row_spec = pl.BlockSpec((ROW_TILE, RWKV_WIDTH), lambda i, f, l: (i, 0))
    return pl.pallas_call(
        _rwkv_prep_kernel,
        out_shape=(pair_out,) * 9 + (row_out,) * 2,
        grid_spec=pltpu.PrefetchScalarGridSpec(
            num_scalar_prefetch=2, grid=(n_tiles,),
            in_specs=main + prev + nxt + consts,
            out_specs=(pair_spec,) * 9 + (row_spec,) * 2),
        compiler_params=_params(1),
        name="rwkv_prep",
    )(first, last, *([proj] * 12), rp["mu_r"], rp["mu_k"], rp["mu_v"], rp["mu_z"], rp["k_k"], rp["k_a"],
      rp["r_k"], rp["w0"], rp["a0"], rp["w_up"], rp["a_up"], rp["g_up"], rp["ones_bd"])


def _bf(x):
    return x.astype(BF16)


def _mm(a, b):
    return jnp.dot(_bf(a), _bf(b), preferred_element_type=F32)


def _mm_nt(a, b):
    return lax.dot_general(_bf(a), _bf(b), (((1,), (1,)), ((), ())), preferred_element_type=F32)


def _mm_tn(a, b):
    return jnp.dot(_bf(a.T), _bf(b), preferred_element_type=F32)


def _scan_blocks(units):
    c = SCAN_CHUNK
    two = 2 * c
    t_i = lax.broadcasted_iota(jnp.int32, (c, c), 0)
    s_i = lax.broadcasted_iota(jnp.int32, (c, c), 1)
    lane = lax.broadcasted_iota(jnp.int32, (c, LANES), 1)
    head0 = lane < RWKV_HEAD
    row2 = lax.broadcasted_iota(jnp.int32, (two, two), 0)
    col2 = lax.broadcasted_iota(jnp.int32, (two, two), 1)
    same_head = (row2 < c) == (col2 < c)
    eye = jnp.where(row2 == col2, 1.0, 0.0)
    rev = [u[7] for u in units]
    tri = {d: jnp.where((s_i >= t_i) if d else (s_i <= t_i), 1.0, 0.0).astype(BF16) for d in set(rev)}
    strict = {d: same_head & ((col2 > row2) if d else (col2 < row2)) for d in set(rev)}
    incl = {d: same_head & ((col2 >= row2) if d else (col2 <= row2)) for d in set(rev)}

    def stack(x):
        return jnp.concatenate([jnp.where(head0, x, 0.0), jnp.where(head0, 0.0, x)], axis=0)

    def each(fn, *cols):
        return [fn(*args) for args in zip(*cols)]

    s, r, v, kk, lw, b, kd = ([u[i] for u in units] for i in range(7))
    hi = each(_bf, lw)
    mid = each(lambda x, h: _bf(x - h.astype(F32)), lw, hi)
    lo = each(lambda x, h, m: _bf(x - h.astype(F32) - m.astype(F32)), lw, hi, mid)
    parts = each(lambda d, h, m, l: jnp.dot(tri[d], jnp.concatenate([h, m, l], axis=1),
                                            preferred_element_type=F32), rev, hi, mid, lo)
    cum = each(lambda x: x[:, 0:LANES] + x[:, LANES:2 * LANES] + x[:, 2 * LANES:3 * LANES], parts)
    total = each(lambda d, x: x[0:1, :] if d else x[c - 1:c, :], rev, cum)
    a_s = each(lambda x, cm, l: stack(-x * jnp.exp(cm - l)), kk, cum, lw)
    r_s = each(lambda x, cm: stack(x * jnp.exp(cm)), r, cum)
    g_inv = each(lambda cm: jnp.exp(-cm), cum)
    g_end = each(lambda tt, cm: jnp.exp(tt - cm), total, cum)
    b_s = each(lambda x, g: stack(x * g), b, g_inv)
    k_s = each(lambda x, g: stack(x * g), kd, g_inv)
    be_s = each(lambda x, g: stack(x * g), b, g_end)
    ke_s = each(lambda x, g: stack(x * g), kd, g_end)
    v_s = each(stack, v)

    a_ab = each(lambda d, x, y: jnp.where(strict[d], _mm_nt(x, y), 0.0), rev, a_s, b_s)
    a_ak = each(lambda d, x, y: jnp.where(strict[d], _mm_nt(x, y), 0.0), rev, a_s, k_s)
    a_rb = each(lambda d, x, y: jnp.where(incl[d], _mm_nt(x, y), 0.0), rev, r_s, b_s)
    a_rk = each(lambda d, x, y: jnp.where(incl[d], _mm_nt(x, y), 0.0), rev, r_s, k_s)

    inv = each(lambda x: eye + x, a_ab)
    power = a_ab
    w = each(_mm, a_ak, v_s)
    for _ in range(int(np.log2(c)) - 1):
        power = each(_mm, power, power)
        inv = each(lambda i, pw: i + _mm(pw, i), inv, power)

    p = each(_mm, inv, a_s)
    q = each(_mm, inv, w)
    qv = each(lambda x, y: jnp.concatenate([x, y], axis=0), q, v_s)
    g_mat = each(lambda x, y, z: x + _mm(y, z), r_s, a_rb, p)
    y0 = each(lambda x, y, z: _mm(jnp.concatenate([x, y], axis=1), z), a_rb, a_rk, qv)
    m_low = each(_mm_tn, be_s, p)
    n_t = each(lambda x, y, z: _mm_tn(x, jnp.concatenate([y, z], axis=0)), qv, be_s, ke_s)

    y = each(lambda g, st, z: _mm_nt(g, st) + z, g_mat, s, y0)
    s_new = each(lambda tt, st, m, n: jnp.exp(tt) * st + _mm_nt(st, m) + n, total, s, m_low, n_t)
    return s_new, each(lambda x: x[0:c, :] + x[c:two, :], y)


def _rwkv_scan_kernel(r_ref, v_ref, kk_ref, lwf_ref, lwb_ref, bf_ref, bb_ref, kdf_ref, kdb_ref,
                      bonus_ref, gate_ref, lnw_ref, lnb_ref, ones_ref, sf0_ref, sb0_ref,
                      o_ref, sf_o, sb_o, yf, yb, *, unroll):
    n_pairs, t = r_ref.shape[0], r_ref.shape[1]
    nc = t // SCAN_CHUNK
    sf_o[...] = sf0_ref[...]
    sb_o[...] = sb0_ref[...]

    def body(ci, carry):
        units, sinks = [], []
        for p in range(n_pairs):
            for reverse, lw_ref, b_ref, kd_ref, s_ref, y_ref in ((False, lwf_ref, bf_ref, kdf_ref, sf_o, yf),
                                                                (True, lwb_ref, bb_ref, kdb_ref, sb_o, yb)):
                cc = (nc - 1 - ci) if reverse else ci
                rows = pl.ds(pl.multiple_of(cc * SCAN_CHUNK, SCAN_CHUNK), SCAN_CHUNK)
                units.append((s_ref[p], r_ref[p, rows, :], v_ref[p, rows, :], kk_ref[p, rows, :],
                              lw_ref[p, rows, :], b_ref[p, rows, :], kd_ref[p, rows, :], reverse))
                sinks.append((s_ref, y_ref, p, rows))
        s_new, y = _scan_blocks(units)
        for (s_ref, y_ref, p, rows), s_val, y_val in zip(sinks, s_new, y):
            s_ref[p] = s_val
            y_ref[p, rows, :] = y_val
        return carry

    lax.fori_loop(0, nc, body, 0, unroll=unroll)

    ones_bd = ones_ref[...]
    for p in range(n_pairs):
        cols = slice(p * LANES, (p + 1) * LANES)
        y = yf[p] + yb[p]
        mean = jnp.dot(y, ones_bd, precision=HIGHEST, preferred_element_type=F32) * (1.0 / RWKV_HEAD)
        dev = y - mean
        var = jnp.dot(dev * dev, ones_bd, precision=HIGHEST, preferred_element_type=F32) * (1.0 / RWKV_HEAD)
        yn = dev * lax.rsqrt(var + LNX_EPS) * lnw_ref[:, cols] + lnb_ref[:, cols]
        o_ref[:, cols] = ((yn + bonus_ref[:, cols]) * gate_ref[:, cols]).astype(o_ref.dtype)


def _rwkv_scan(prep, lnx_w, lnx_b, ones_bd, s0f, s0b, n_seq, t, row_base, pairs_per_step, unroll):
    r, v, kk, lwf, lwb, bf, bb, kdf, kdb, bonus, gate = prep
    base = row_base // t
    pp = pairs_per_step
    width = pp * LANES
    pair_spec = pl.BlockSpec((pp, t, LANES), lambda s, p: (p, base + s, 0))
    row_spec = pl.BlockSpec((t, width), lambda s, p: (base + s, p))
    vec_spec = pl.BlockSpec((1, width), lambda s, p: (0, p))
    s_spec = pl.BlockSpec((None, pp, LANES, LANES), lambda s, p: (s, p, 0, 0))
    s_shape = jax.ShapeDtypeStruct((n_seq, RWKV_PAIRS, LANES, LANES), F32)
    return pl.pallas_call(
        functools.partial(_rwkv_scan_kernel, unroll=unroll),
        out_shape=(jax.ShapeDtypeStruct((n_seq * t, RWKV_WIDTH), BF16), s_shape, s_shape),
        grid=(n_seq, RWKV_PAIRS // pp),
        in_specs=[pair_spec] * 9 + [row_spec, row_spec, vec_spec, vec_spec,
                                    pl.BlockSpec((LANES, LANES), lambda s, p: (0, 0)), s_spec, s_spec],
        out_specs=(pl.BlockSpec((t, width), lambda s, p: (s, p)), s_spec, s_spec),
        scratch_shapes=[pltpu.VMEM((pp, t, LANES), F32), pltpu.VMEM((pp, t, LANES), F32)],
        compiler_params=_params(2),
        name="rwkv_scan",
    )(r, v, kk, lwf, lwb, bf, bb, kdf, kdb, bonus, gate, lnx_w, lnx_b, ones_bd, s0f, s0b)


def _state_to_blockdiag(s):
    n = s.shape[0]
    h = s.reshape(n, RWKV_PAIRS, 2, RWKV_HEAD, RWKV_HEAD)
    z = jnp.zeros_like(h[:, :, 0])
    top = jnp.concatenate([h[:, :, 0], z], axis=-1)
    bot = jnp.concatenate([z, h[:, :, 1]], axis=-1)
    return jnp.concatenate([top, bot], axis=-2)


def _blockdiag_to_state(h):
    n = h.shape[0]
    h0 = h[:, :, :RWKV_HEAD, :RWKV_HEAD]
    h1 = h[:, :, RWKV_HEAD:, RWKV_HEAD:]
    return jnp.stack([h0, h1], axis=2).reshape(n, 2 * RWKV_PAIRS, RWKV_HEAD, RWKV_HEAD)


def _out_proj_kernel(tm_ref, attc_ref, attl_ref, rwc_ref, rwl_ref, sg_ref, w_ref, x_ref, g_ref, o_ref, *, n_ctx_tiles):
    def project(att_ref, rw_ref):
        acc = jnp.dot(att_ref[...], w_ref[0:ATT_WIDTH, :], preferred_element_type=F32)
        acc += jnp.dot(rw_ref[...], w_ref[ATT_WIDTH:ATT_WIDTH + RWKV_WIDTH, :], preferred_element_type=F32)
        acc += jnp.dot(sg_ref[...], w_ref[ATT_WIDTH + RWKV_WIDTH:, :], preferred_element_type=F32)
        o_ref[...] = x_ref[...] + g_ref[0] * acc

    is_ctx = pl.program_id(0) < n_ctx_tiles

    @pl.when(is_ctx)
    def _():
        project(attc_ref, rwc_ref)

    @pl.when(jnp.logical_not(is_ctx))
    def _():
        project(attl_ref, rwl_ref)


def _out_proj(att, rw, sg, w_bf, x, mod48, lay):
    n = x.shape[0]
    tm, tn = 512, 512
    tile_mod = jnp.asarray(_tile_mod_rows(lay, tm))
    nct = lay.n_ctx * lay.t_ctx // tm
    nlt = lay.n_lat * lay.t_lat // tm

    def ctx_rows(i, j, t):
        return (jnp.minimum(i, nct - 1), 0)

    def lat_rows(i, j, t):
        return (jnp.clip(i - nct, 0, nlt - 1), 0)

    return pl.pallas_call(
        functools.partial(_out_proj_kernel, n_ctx_tiles=nct),
        out_shape=jax.ShapeDtypeStruct((n, D_MODEL), F32),
        grid_spec=pltpu.PrefetchScalarGridSpec(
            num_scalar_prefetch=1, grid=(n // tm, D_MODEL // tn),
            in_specs=[pl.BlockSpec((tm, ATT_WIDTH), ctx_rows),
                      pl.BlockSpec((tm, ATT_WIDTH), lat_rows),
                      pl.BlockSpec((tm, RWKV_WIDTH), ctx_rows),
                      pl.BlockSpec((tm, RWKV_WIDTH), lat_rows),
                      pl.BlockSpec((tm, SGU_WIDTH), lambda i, j, t: (i, 0)),
                      pl.BlockSpec((D_MODEL, tn), lambda i, j, t: (0, j)),
                      pl.BlockSpec((tm, tn), lambda i, j, t: (i, j)),
                      pl.BlockSpec((1, 1, tn), lambda i, j, t: (t[i] * 6 + 2, 0, j))],
            out_specs=pl.BlockSpec((tm, tn), lambda i, j, t: (i, j))),
        compiler_params=_params(2),
        name="out_proj",
    )(tile_mod, att[0], att[1], rw[0], rw[1], sg, w_bf, x, mod48)


def _dispatch_kernel(src_ref, tv_ref, h_hbm, o_ref, sem):
    i = pl.program_id(0)
    base = i * MOE_TM

    def row_copy(j):
        return pltpu.make_async_copy(h_hbm.at[pl.ds(src_ref[base + j], 1)], o_ref.at[pl.ds(j, 1)], sem)

    def issue(j, carry):
        row_copy(j).start()
        return carry

    def drain(j, carry):
        row_copy(j).wait()
        return carry

    @pl.when(tv_ref[i] == 1)
    def _():
        lax.fori_loop(0, MOE_TM, issue, 0, unroll=8)
        lax.fori_loop(0, MOE_TM, drain, 0, unroll=8)

    @pl.when(tv_ref[i] == 0)
    def _():
        o_ref[...] = jnp.zeros_like(o_ref)


def _dispatch(h_packed, src, tv):
    r = src.shape[0]
    half = h_packed.shape[1]
    return pl.pallas_call(
        _dispatch_kernel,
        out_shape=jax.ShapeDtypeStruct((r, half), jnp.uint32),
        grid_spec=pltpu.PrefetchScalarGridSpec(
            num_scalar_prefetch=2, grid=(r // MOE_TM,),
            in_specs=[pl.BlockSpec(memory_space=pl.ANY)],
            out_specs=pl.BlockSpec((MOE_TM, half), lambda i, s, v: (i, 0)),
            scratch_shapes=[pltpu.SemaphoreType.DMA(())]),
        compiler_params=_params(1),
        name="moe_dispatch",
    )(src, tv, h_packed)


def _unpack_bf16_pair(words):
    lo = lax.bitcast_convert_type(words << 16, F32).astype(BF16)
    hi = lax.bitcast_convert_type(words & jnp.uint32(0xFFFF0000), F32).astype(BF16)
    return lo, hi


def _stream_expert_weights(te_ref, tf_ref, tn_ref, tl_ref, w_hbm, layer, col_starts, width, stage, w_bf, sem, cnt_ref):
    j, i = pl.program_id(0), pl.program_id(1)
    nj = pl.num_programs(0)

    def copies(expert, jj, slot):
        out = []
        for part, start in enumerate(col_starts):
            cols = pl.ds(pl.multiple_of(start + jj * width, width), width)
            out.append(pltpu.make_async_copy(w_hbm.at[layer, expert, :, cols], stage.at[slot, part], sem.at[slot, part]))
        return out

    @pl.when(tf_ref[i] == 1)
    def _():
        @pl.when(jnp.logical_and(j == 0, i == 0))
        def _():
            cnt_ref[0] = 0
            for c in copies(te_ref[0], 0, 0):
                c.start()

        slot = cnt_ref[0] & 1
        for c in copies(te_ref[i], j, slot):
            c.wait()
        next_j = j + tl_ref[i]

        @pl.when(next_j < nj)
        def _():
            for c in copies(tn_ref[i], next_j, 1 - slot):
                c.start()

        for part in range(len(col_starts)):
            w_bf[part] = stage[slot, part].astype(BF16)
        cnt_ref[0] = cnt_ref[0] + 1


def _moe_up_kernel(te_ref, tx_ref, tv_ref, tf_ref, tn_ref, tl_ref, x_ref, w_hbm, bg_ref, bl_ref, o_ref,
                   stage, w_bf, sem, cnt_ref, *, layer):
    i = pl.program_id(1)
    half = D_MODEL // 2
    _stream_expert_weights(te_ref, tf_ref, tn_ref, tl_ref, w_hbm, layer, (0, EXPERT_FF), MOE_TN1,
                           stage, w_bf, sem, cnt_ref)

    @pl.when(tv_ref[i] == 1)
    def _():
        lo, hi = _unpack_bf16_pair(x_ref[...])
        z_glu = (jnp.dot(lo, w_bf[0, 0:half, :], preferred_element_type=F32)
                 + jnp.dot(hi, w_bf[0, half:, :], preferred_element_type=F32) + bg_ref[...])
        z_lin = (jnp.dot(lo, w_bf[1, 0:half, :], preferred_element_type=F32)
                 + jnp.dot(hi, w_bf[1, half:, :], preferred_element_type=F32) + bl_ref[...])
        z_glu = jnp.minimum(z_glu, SWIGLU_LIMIT)
        z_lin = jnp.clip(z_lin, -SWIGLU_LIMIT, SWIGLU_LIMIT)
        o_ref[...] = (z_glu * jax.nn.sigmoid(SWIGLU_ALPHA * z_glu) * (z_lin + 1.0)).astype(o_ref.dtype)

    @pl.when(tv_ref[i] == 0)
    def _():
        o_ref[...] = jnp.zeros_like(o_ref)


def _moe_down_kernel(te_ref, tx_ref, tv_ref, tf_ref, tn_ref, tl_ref, a_ref, w_hbm, b_ref, o_ref,
                     stage, w_bf, sem, cnt_ref, *, layer):
    i = pl.program_id(1)
    _stream_expert_weights(te_ref, tf_ref, tn_ref, tl_ref, w_hbm, layer, (0,), MOE_TN2, stage, w_bf, sem, cnt_ref)

    @pl.when(tv_ref[i] == 1)
    def _():
        y = jnp.dot(a_ref[...], w_bf[0], preferred_element_type=F32) + b_ref[...]
        o_ref[...] = y.astype(o_ref.dtype)

    @pl.when(tv_ref[i] == 0)
    def _():
        o_ref[...] = jnp.zeros_like(o_ref)


def _moe_experts(xs, tiles, w1, b1, w2, b2, layer):
    te, tx, tv, tf, tn, tl = tiles
    r = xs.shape[0]
    n_tiles = r // MOE_TM
    nj1 = EXPERT_FF // MOE_TN1
    hbm = pl.BlockSpec(memory_space=pl.ANY)
    act = pl.pallas_call(
        functools.partial(_moe_up_kernel, layer=layer),
        out_shape=jax.ShapeDtypeStruct((r, EXPERT_FF), BF16),
        grid_spec=pltpu.PrefetchScalarGridSpec(
            num_scalar_prefetch=6, grid=(nj1, n_tiles),
            in_specs=[pl.BlockSpec((MOE_TM, D_MODEL // 2), lambda j, i, te, tx, *_: (tx[i], 0)),
                      hbm,
                      pl.BlockSpec((None, 1, MOE_TN1), lambda j, i, te, *_: (layer * N_EXPERTS + te[i], 0, j)),
                      pl.BlockSpec((None, 1, MOE_TN1), lambda j, i, te, *_: (layer * N_EXPERTS + te[i], 0, nj1 + j))],
            out_specs=pl.BlockSpec((MOE_TM, MOE_TN1), lambda j, i, *_: (i, j)),
            scratch_shapes=[pltpu.VMEM((2, 2, D_MODEL, MOE_TN1), F32), pltpu.VMEM((2, D_MODEL, MOE_TN1), BF16),
                            pltpu.SemaphoreType.DMA((2, 2)), pltpu.SMEM((1,), jnp.int32)]),
        compiler_params=_params(2, MOE_VMEM_LIMIT),
        name="moe_up",
    )(te, tx, tv, tf, tn, tl, xs, w1, b1, b1)
    nj2 = D_MODEL // MOE_TN2
    return pl.pallas_call(
        functools.partial(_moe_down_kernel, layer=layer),
        out_shape=jax.ShapeDtypeStruct((r, D_MODEL), F32),
        grid_spec=pltpu.PrefetchScalarGridSpec(
            num_scalar_prefetch=6, grid=(nj2, n_tiles),
            in_specs=[pl.BlockSpec((MOE_TM, EXPERT_FF), lambda j, i, te, tx, *_: (tx[i], 0)),
                      hbm,
                      pl.BlockSpec((None, 1, MOE_TN2), lambda j, i, te, *_: (layer * N_EXPERTS + te[i], 0, j))],
            out_specs=pl.BlockSpec((MOE_TM, MOE_TN2), lambda j, i, *_: (i, j)),
            scratch_shapes=[pltpu.VMEM((2, 1, EXPERT_FF, MOE_TN2), F32), pltpu.VMEM((1, EXPERT_FF, MOE_TN2), BF16),
                            pltpu.SemaphoreType.DMA((2, 1)), pltpu.SMEM((1,), jnp.int32)]),
        compiler_params=_params(2, MOE_VMEM_LIMIT),
        name="moe_down",
    )(te, tx, tv, tf, tn, tl, act, w2, b2)


def _route(sel, idx4):
    n = sel.shape[0]
    r_max = n * TOP_K + N_EXPERTS * MOE_TM
    n_tiles = r_max // MOE_TM
    sel_i = sel.astype(jnp.int32)
    counts = jnp.sum(sel_i, axis=0)
    tiles_per = (counts + MOE_TM - 1) // MOE_TM
    tile_end = jnp.cumsum(tiles_per)
    offsets = (tile_end - tiles_per) * MOE_TM
    pos_all = offsets[None, :] + jnp.cumsum(sel_i, axis=0) - sel_i
    experts = jnp.arange(N_EXPERTS, dtype=jnp.int32)
    pos4 = jnp.sum(jnp.where(idx4[:, :, None] == experts[None, None, :], pos_all[:, None, :], 0), axis=-1)

    n_used = tile_end[-1]
    tile_ids = jnp.arange(n_tiles, dtype=jnp.int32)
    tx = jnp.minimum(tile_ids, n_used - 1)
    te = jnp.minimum(jnp.sum((tile_end[None, :] <= tx[:, None]).astype(jnp.int32), axis=1), N_EXPERTS - 1)
    tv = (tile_ids < n_used).astype(jnp.int32)
    tf = jnp.concatenate([jnp.ones((1,), jnp.int32), (te[1:] != te[:-1]).astype(jnp.int32)])

    te = te.astype(jnp.int32)
    later_first = (tile_ids[None, :] > tile_ids[:, None]) & (tf[None, :] == 1) & (tv[None, :] == 1)
    next_first = jnp.min(jnp.where(later_first, tile_ids[None, :], n_tiles), axis=1)
    tl = (next_first == n_tiles).astype(jnp.int32)
    tn = jnp.sum(jnp.where(tile_ids[None, :] == jnp.where(tl == 1, 0, next_first)[:, None], te[None, :], 0), axis=1)

    toks = jnp.repeat(jnp.arange(n, dtype=jnp.int32), TOP_K)
    src = jnp.zeros((r_max,), jnp.int32).at[pos4.reshape(-1)].set(toks, unique_indices=True)
    return src, pos4, (te, tx, tv, tf, tn.astype(jnp.int32), tl)


COMBINE_ROWS = TOP_K * ROW_TILE


def _expert_mix(pos_ref, x_ref, gate_ref, g_ref, ys_hbm, buf, sem):
    base = pl.program_id(0) * COMBINE_ROWS

    def row_copy(j):
        return pltpu.make_async_copy(ys_hbm.at[pl.ds(pos_ref[base + j], 1)], buf.at[pl.ds(j, 1)], sem)

    def issue(j, carry):
        row_copy(j).start()
        return carry

    def drain(j, carry):
        row_copy(j).wait()
        return carry

    lax.fori_loop(0, COMBINE_ROWS, issue, 0, unroll=8)
    lax.fori_loop(0, COMBINE_ROWS, drain, 0, unroll=8)
    y = gate_ref[:, 0:1] * buf[0:ROW_TILE, :]
    for k in range(1, TOP_K):
        y += gate_ref[:, k:k + 1] * buf[k * ROW_TILE:(k + 1) * ROW_TILE, :]
    return x_ref[...] + g_ref[0] * y


def _resid_kernel(tm_ref, pos_ref, x_ref, gate_ref, g_ref, ys_hbm, o_ref, buf, sem):
    o_ref[...] = _expert_mix(pos_ref, x_ref, gate_ref, g_ref, ys_hbm, buf, sem)


def _resid_norm_kernel(tm_ref, pos_ref, x_ref, gate_ref, g_ref, fn_ref, ys_hbm, o_ref, buf, sem):
    x = _expert_mix(pos_ref, x_ref, gate_ref, g_ref, ys_hbm, buf, sem)
    o_ref[...] = x * lax.rsqrt(jnp.mean(x * x, axis=-1, keepdims=True) + NORM_EPS) * fn_ref[...]


def _gated_residual(x, out_sorted, pos4, gates, mod48, lay, final_norm=None):
    n = x.shape[0]
    nt = n // ROW_TILE
    tile_mod = jnp.asarray(_tile_mod_rows(lay, ROW_TILE))
    pos_flat = pos4.reshape(nt, ROW_TILE, TOP_K).transpose(0, 2, 1).reshape(-1)
    tile = pl.BlockSpec((ROW_TILE, D_MODEL), lambda i, t, q: (i, 0))
    in_specs = [tile, pl.BlockSpec((ROW_TILE, LANES), lambda i, t, q: (i, 0)),
                pl.BlockSpec((1, 1, D_MODEL), lambda i, t, q: (t[i] * 6 + 5, 0, 0))]
    args = [tile_mod, pos_flat, x, gates, mod48]
    kern = _resid_kernel
    if final_norm is not None:
        in_specs.append(pl.BlockSpec((1, D_MODEL), lambda i, t, q: (0, 0)))
        args.append(final_norm)
        kern = _resid_norm_kernel
    in_specs.append(pl.BlockSpec(memory_space=pl.ANY))
    args.append(out_sorted)
    return pl.pallas_call(
        kern,
        out_shape=jax.ShapeDtypeStruct((n, D_MODEL), F32),
        grid_spec=pltpu.PrefetchScalarGridSpec(
            num_scalar_prefetch=2, grid=(nt,), in_specs=in_specs, out_specs=tile,
            scratch_shapes=[pltpu.VMEM((COMBINE_ROWS, D_MODEL), F32), pltpu.SemaphoreType.DMA(())]),
        compiler_params=_params(1),
        name="gated_residual",
    )(*args)


def _block_ones():
    idx = np.arange(LANES) // RWKV_HEAD
    return jnp.asarray((idx[:, None] == idx[None, :]).astype(np.float32))


def _layer_weights(l, w_in, w_out, rwkv_mu, w0, w_up, a0, a_up, g_up, k_k, k_a, r_k, sgu_w, sgu_b, w_router, b_router):
    w_sgu = w_in[l][None, :, SGU_IN_START:].astype(BF16)
    mu = rwkv_mu[l]
    zeros_lora = jnp.zeros((DECAY_LORA, RWKV_WIDTH), F32)

    def blockdiag(up):
        return jnp.concatenate([jnp.concatenate([up[0], zeros_lora], axis=1),
                                jnp.concatenate([zeros_lora, up[1]], axis=1)], axis=0).astype(BF16)

    rp = {
        "mu_r": mu[None, 0:RWKV_WIDTH], "mu_k": mu[None, RWKV_WIDTH:2 * RWKV_WIDTH],
        "mu_v": mu[None, 2 * RWKV_WIDTH:3 * RWKV_WIDTH],
        "mu_z": jnp.pad(mu[3 * RWKV_WIDTH:], (0, Z_WIDTH - (RWKV_IN - 3 * RWKV_WIDTH)))[None],
        "k_k": k_k[l][None], "k_a": k_a[l][None], "r_k": r_k[l].reshape(1, RWKV_WIDTH),
        "w0": w0[l], "a0": a0[l], "w_up": blockdiag(w_up[l]), "a_up": blockdiag(a_up[l]),
        "g_up": jnp.pad(g_up[l], ((0, 2 * LANES - GATE_LORA), (0, 0))).astype(BF16),
        "ones_bd": _block_ones(),
    }
    wr = jnp.pad(w_router[l], ((0, 0), (0, LANES - N_EXPERTS)))
    br = jnp.pad(b_router[l], (0, LANES - N_EXPERTS))[None]
    return {"w_sgu": w_sgu, "w_out": w_out[l].astype(BF16), "rp": rp,
            "sgu_w": sgu_w[l].astype(BF16), "sgu_bt": sgu_b[l].T, "router": (wr, br)}


def _layer(x, l, lay, mod48, lw, p, cache_k, cache_v, h0_lat, rope):
    n_ctx_rows = lay.n_ctx * lay.t_ctx
    h = _norm_mod(x, p["norm_mix"][l][None], mod48, 0, lay)
    tm = 1024 if x.shape[0] % 1024 == 0 else 512
    proj = _matmul(h, p["w_in"], l, P_WIDTH, tm, 512)
    proj_sgu = _matmul(h, lw["w_sgu"], 0, 2 * SGU_WIDTH, tm, 512)

    qn, kn = p["q_norm"][l][None], p["k_norm"][l][None]
    att_ctx, k_ctx = _attention_ctx(proj, qn, kn, lay)
    att_lat = _attention_lat(proj, cache_k, cache_v, rope[0], rope[1], qn, kn, l, lay)
    v_ctx = proj[:n_ctx_rows, COL_V:COL_V + KV_WIDTH]

    prep = _rwkv_prep(proj, lw["rp"], lay)
    lnw, lnb, ones_bd = p["lnx_w"][l][None], p["lnx_b"][l][None], lw["rp"]["ones_bd"]
    zeros_h = jnp.zeros((lay.n_ctx, RWKV_PAIRS, LANES, LANES), F32)
    rw_ctx, hf_ctx, hb_ctx = _rwkv_scan(prep, lnw, lnb, ones_bd, zeros_h, zeros_h, lay.n_ctx, lay.t_ctx, 0,
                                        pairs_per_step=SCAN_PAIRS_CTX, unroll=1)
    rw_lat, _, _ = _rwkv_scan(prep, lnw, lnb, ones_bd, h0_lat[0], h0_lat[1], lay.n_lat, lay.t_lat, n_ctx_rows,
                              pairs_per_step=SCAN_PAIRS_LAT, unroll=1)

    sg = _sgu(proj_sgu, lw["sgu_w"], lw["sgu_bt"], p["sgu_norm"][l][None])
    x = _out_proj((att_ctx, att_lat), (rw_ctx, rw_lat), sg, lw["w_out"], x, mod48, lay)

    h2, gates, idx, sel = _norm_mod(x, p["norm_ffn"][l][None], mod48, 3, lay, router=lw["router"])
    src, pos4, tiles = _route(sel[:, :N_EXPERTS], idx[:, :TOP_K])
    xs = _dispatch(h2, src, tiles[2])
    out_sorted = _moe_experts(xs, tiles, p["w_exp1"], p["b_exp1r"], p["w_exp2"], p["b_exp2r"], l)
    final = p["final_norm"][None] if l == DEPTH - 1 else None
    x = _gated_residual(x, out_sorted, pos4, gates, mod48, lay, final)
    return x, k_ctx, v_ctx, hf_ctx, hb_ctx


def _forward(lay, x_prompt, x_sample, cache_k, cache_v, state_wkv_fwd, state_wkv_bwd, c, c_ctx, p):
    n_ctx_rows = lay.n_ctx * lay.t_ctx
    x = jnp.concatenate([x_prompt.reshape(n_ctx_rows, D_MODEL), x_sample.reshape(lay.n_lat * lay.t_lat, D_MODEL)])
    cond8 = jnp.zeros((8, D_MODEL), F32).at[0].set(c_ctx).at[1:1 + lay.n_lat].set(c)
    rope = _rope_tables(lay.t_lat)
    ck = cache_k.reshape(lay.n_lat, DEPTH, lay.past, KV_WIDTH)
    cv = cache_v.reshape(lay.n_lat, DEPTH, lay.past, KV_WIDTH)
    p = dict(p)
    p["b_exp1r"] = p["b_exp1"].reshape(DEPTH * N_EXPERTS, 1, 2 * EXPERT_FF)
    p["b_exp2r"] = p["b_exp2"].reshape(DEPTH * N_EXPERTS, 1, D_MODEL)
    ks, vs, sfs, sbs = [], [], [], []
    for l in range(DEPTH):
        lw = _layer_weights(l, p["w_in"], p["w_out"], p["rwkv_mu"], p["w0"], p["w_up"], p["a0"], p["a_up"],
                            p["g_up"], p["k_k"], p["k_a"], p["r_k"], p["sgu_w"], p["sgu_b"],
                            p["w_router"], p["b_router"])
        mod = _modulation(cond8, p["w_mod"], p["b_mod"], l)
        mod48 = mod.reshape(8 * 6, 1, D_MODEL)
        h0_lat = (_state_to_blockdiag(state_wkv_fwd[:, l]), _state_to_blockdiag(state_wkv_bwd[:, l]))
        x, k_l, v_l, hf, hb = _layer(x, l, lay, mod48, lw, p, ck, cv, h0_lat, rope)
        ks.append(k_l.reshape(lay.n_ctx, lay.t_ctx, ATT_KV_HEADS, HEAD_DIM))
        vs.append(v_l.reshape(lay.n_ctx, lay.t_ctx, ATT_KV_HEADS, HEAD_DIM))
        sfs.append(_blockdiag_to_state(hf))
        sbs.append(_blockdiag_to_state(hb))
    y_prompt = x[:n_ctx_rows].reshape(x_prompt.shape)
    y_sample = x[n_ctx_rows:].reshape(x_sample.shape)
    return (y_prompt, y_sample, jnp.stack(ks, axis=1), jnp.stack(vs, axis=1),
            jnp.stack(sfs, axis=1), jnp.stack(sbs, axis=1))


def kernel(x_prompt, x_sample, cache_k, cache_v, state_wkv_fwd, state_wkv_bwd, c, c_ctx, norm_mix, norm_ffn, w_mod, b_mod, w_in, w_out, q_norm, k_norm, rwkv_mu, w0, w_up, a0, a_up, g_up, k_k, k_a, r_k, lnx_w, lnx_b, sgu_norm, sgu_w, sgu_b, w_router, b_router, w_exp1, b_exp1, w_exp2, b_exp2, final_norm):
    lay = Layout(n_ctx=x_prompt.shape[0], t_ctx=x_prompt.shape[1], n_lat=x_sample.shape[0],
                 t_lat=x_sample.shape[1], past=cache_k.shape[2])
    p = dict(norm_mix=norm_mix, norm_ffn=norm_ffn, w_mod=w_mod, b_mod=b_mod, w_in=w_in, w_out=w_out,
             q_norm=q_norm, k_norm=k_norm, rwkv_mu=rwkv_mu, w0=w0, w_up=w_up, a0=a0, a_up=a_up, g_up=g_up,
             k_k=k_k, k_a=k_a, r_k=r_k, lnx_w=lnx_w, lnx_b=lnx_b, sgu_norm=sgu_norm, sgu_w=sgu_w, sgu_b=sgu_b,
             w_router=w_router, b_router=b_router, w_exp1=w_exp1, b_exp1=b_exp1, w_exp2=w_exp2, b_exp2=b_exp2,
             final_norm=final_norm)
    return _forward(lay, x_prompt, x_sample, cache_k, cache_v, state_wkv_fwd, state_wkv_bwd, c, c_ctx, p)
```
